```python
import jax, jax.numpy as jnp
from jax import lax
import numpy as np

D_MODEL = 1024
BATCH = 16
SEQ = 4096
DEPTH = 1
DEC_BATCH = 8
DEC_SEQ = 16
PAST_LEN = 2048

CHUNK = 64
RET_HEADS = 4
RET_DK = D_MODEL // RET_HEADS
RET_DV = D_MODEL // RET_HEADS
HG_DK = 128
HG_HEADS = D_MODEL // HG_DK
HG_DV = D_MODEL // HG_HEADS
HG_BLOCK = 16
D_FF = 4 * D_MODEL
ROPE_BASE = 10000.0
EPS = 1e-6

RET_QK = RET_HEADS * RET_DK
RET_V = RET_HEADS * RET_DV
HG_QK = HG_HEADS * HG_DK
HG_V = HG_HEADS * HG_DV
IN_SIZES = (RET_QK, RET_QK, RET_V, RET_V, HG_QK, HG_QK, HG_V, HG_V, D_MODEL, D_MODEL)
D_IN = RET_QK * 2 + RET_V * 2 + HG_QK * 2 + HG_V * 2 + D_MODEL * 2

kernel_name = 'retnet_hgrn2_gated_parallel_stream'


def _rms_norm(x, g):
    xf = x.astype(jnp.float32)
    y = xf * lax.rsqrt(jnp.mean(xf * xf, axis=-1, keepdims=True) + EPS)
    return (y * g.astype(jnp.float32)).astype(x.dtype)


def _head_rms(x):
    return x * lax.rsqrt(jnp.mean(x * x, axis=-1, keepdims=True) + EPS)


def _split_cols(z):
    offs = np.cumsum(IN_SIZES)[:-1].tolist()
    return jnp.split(z, offs, axis=-1)


def _ret_log_gamma():
    g = 1.0 - np.exp(np.linspace(np.log(1.0 / 32), np.log(1.0 / 512), RET_HEADS))
    return jnp.asarray(np.log(g), dtype=jnp.float32)


def _rope(x, pos):
    d = x.shape[-1]
    inv = ROPE_BASE ** (-jnp.arange(0, d, 2, dtype=jnp.float32) / d)
    ang = pos.astype(jnp.float32)[:, None] * inv[None, :]
    cos = jnp.cos(ang)[None, :, None, :]
    sin = jnp.sin(ang)[None, :, None, :]
    x1, x2 = x[..., : d // 2], x[..., d // 2:]
    return jnp.concatenate([x1 * cos - x2 * sin, x1 * sin + x2 * cos], axis=-1)


def _retention(q, k, v, s0, chunk):
    B, T, H, dk = q.shape
    dv = v.shape[-1]
    n = T // chunk
    lg = _ret_log_gamma()
    idx = jnp.arange(chunk, dtype=jnp.float32)
    dmat = jnp.exp(jnp.abs(idx[:, None] - idx[None, :])[None] * lg[:, None, None])
    xi = jnp.exp((idx + 1.0)[:, None] * lg[None, :])
    zeta = jnp.exp((chunk - 1.0 - idx)[:, None] * lg[None, :])
    dec_c = jnp.exp(chunk * lg)
    qc = q.reshape(B, n, chunk, H, dk)
    kc = k.reshape(B, n, chunk, H, dk)
    vc = v.reshape(B, n, chunk, H, dv)
    scores = jnp.einsum('bnqhd,bnkhd->bnhqk', qc, kc) * dmat[None, None]
    intra = jnp.einsum('bnhqk,bnkhe->bnqhe', scores, vc)
    qx = jnp.moveaxis(qc * xi[None, None, :, :, None], 1, 0)
    kz = jnp.moveaxis(kc * zeta[None, None, :, :, None], 1, 0)
    vs = jnp.moveaxis(vc, 1, 0)

    def step(s, inp):
        q_i, k_i, v_i = inp
        cross = jnp.einsum('bchd,bhde->bche', q_i, s)
        s_new = dec_c[None, :, None, None] * s + jnp.einsum('bchd,bche->bhde', k_i, v_i)
        return s_new, cross

    s_last, cross = lax.scan(step, s0, (qx, kz, vs))
    out = intra + jnp.moveaxis(cross, 0, 1)
    return out.reshape(B, T, H, dv), s_last


def _hgrn2(q, k, v, logf, s0, block):
    B, T, H, dk = q.shape
    dv = v.shape[-1]
    n = T // block
    to_blocks = lambda a: jnp.moveaxis(a.reshape(B, n, block, H, a.shape[-1]), 1, 0)
    tri = jnp.tril(jnp.ones((block, block), dtype=bool))

    def step(s, inp):
        q_i, k_i, v_i, lf_i = inp
        b = jnp.cumsum(lf_i, axis=1)
        inter = jnp.einsum('bthd,bhde->bthe', q_i * jnp.exp(b), s)
        diff = b[:, :, None] - b[:, None, :]
        diff = jnp.where(tri[None, :, :, None, None], diff, -jnp.inf)
        att = jnp.einsum('bthd,btshd->bhts', q_i, jnp.exp(diff) * k_i[:, None])
        intra = jnp.einsum('bhts,bshe->bthe', att, v_i)
        b_last = b[:, -1]
        k_dec = k_i * jnp.exp(b_last[:, None] - b)
        s_new = jnp.exp(b_last)[..., None] * s + jnp.einsum('bshd,bshe->bhde', k_dec, v_i)
        return s_new, inter + intra

    s_last, out = lax.scan(step, s0, (to_blocks(q), to_blocks(k), to_blocks(v), to_blocks(logf)))
    return jnp.moveaxis(out, 0, 1).reshape(B, T, H, dv), s_last


def _layer(x, pos, s_ret, s_hg, lb, w_in, w_a, w_b, w_out, w_up, w_down,
           g_pre_mix, g_post_mix, g_pre_ffn, g_post_ffn, ret_chunk, hg_block):
    B, T, _ = x.shape
    f32 = jnp.float32
    h = _rms_norm(x, g_pre_mix)
    z = h @ w_in
    q_a, k_a, v_a, g_a, q_b, f_b, i_b, g_b, m_a, m_b = _split_cols(z)

    qa = _rope(q_a.astype(f32).reshape(B, T, RET_HEADS, RET_DK), pos)
    ka = _rope(k_a.astype(f32).reshape(B, T, RET_HEADS, RET_DK), pos) * (RET_DK ** -0.5)
    va = v_a.astype(f32).reshape(B, T, RET_HEADS, RET_DV)
    o_a, s_ret_new = _retention(qa, ka, va, s_ret, ret_chunk)
    o_a = _head_rms(o_a).reshape(B, T, RET_V)
    y_a = (jax.nn.silu(g_a.astype(f32)) * o_a).astype(x.dtype) @ w_a

    lbh = lb.reshape(HG_HEADS, HG_DK)
    f = lbh + (1.0 - lbh) * jax.nn.sigmoid(f_b.astype(f32).reshape(B, T, HG_HEADS, HG_DK))
    qb = q_b.astype(f32).reshape(B, T, HG_HEADS, HG_DK)
    vb = i_b.astype(f32).reshape(B, T, HG_HEADS, HG_DV)
    o_b, s_hg_new = _hgrn2(qb, 1.0 - f, vb, jnp.log(f), s_hg, hg_block)
    o_b = _head_rms(o_b).reshape(B, T, HG_V)
    y_b = (jax.nn.silu(g_b.astype(f32)) * o_b).astype(x.dtype) @ w_b

    mix = jax.nn.sigmoid(m_a) * y_a + jax.nn.sigmoid(m_b) * y_b
    x = x + _rms_norm(mix @ w_out, g_post_mix)

    u = jnp.square(jax.nn.relu(_rms_norm(x, g_pre_ffn) @ w_up))
    x = x + _rms_norm(u @ w_down, g_post_ffn)
    return x, s_ret_new, s_hg_new


def setup_inputs(seed: int = 0) -> dict:
    key = jax.random.key(seed)
    ks = jax.random.split(key, 16)
    f32 = jnp.float32
    nrm = lambda k, shape, s: jax.random.normal(k, shape, f32) * s
    return {
        'x_prompt': nrm(ks[0], (BATCH, SEQ, D_MODEL), 1.0),
        'x_sample': nrm(ks[1], (DEC_BATCH, DEC_SEQ, D_MODEL), 1.0),
        'state_ret': nrm(ks[2], (DEPTH, DEC_BATCH, RET_HEADS, RET_DK, RET_DV), 0.5),
        'state_hgrn': nrm(ks[3], (DEPTH, DEC_BATCH, HG_HEADS, HG_DK, HG_DV), 0.5),
        'lb_logits': nrm(ks[4], (DEPTH + 1, HG_QK), 0.5),
        'w_in': nrm(ks[5], (DEPTH, D_MODEL, D_IN), D_MODEL ** -0.5),
        'w_a': nrm(ks[6], (DEPTH, RET_V, D_MODEL), RET_V ** -0.5),
        'w_b': nrm(ks[7], (DEPTH, HG_V, D_MODEL), HG_V ** -0.5),
        'w_out': nrm(ks[8], (DEPTH, D_MODEL, D_MODEL), D_MODEL ** -0.5),
        'w_up': nrm(ks[9], (DEPTH, D_MODEL, D_FF), D_MODEL ** -0.5),
        'w_down': nrm(ks[10], (DEPTH, D_FF, D_MODEL), D_FF ** -0.5),
        'g_pre_mix': 1.0 + nrm(ks[11], (DEPTH, D_MODEL), 0.05),
        'g_post_mix': 1.0 + nrm(ks[12], (DEPTH, D_MODEL), 0.05),
        'g_pre_ffn': 1.0 + nrm(ks[13], (DEPTH, D_MODEL), 0.05),
        'g_post_ffn': 1.0 + nrm(ks[14], (DEPTH, D_MODEL), 0.05),
    }


def reference(x_prompt, x_sample, state_ret, state_hgrn, lb_logits, w_in, w_a, w_b, w_out,
              w_up, w_down, g_pre_mix, g_post_mix, g_pre_ffn, g_post_ffn):
    f32 = jnp.float32
    lb_all = jnp.cumsum(jax.nn.softmax(lb_logits.astype(f32), axis=0), axis=0)
    B, T, _ = x_prompt.shape
    Bs, Ts, _ = x_sample.shape
    pos_p = jnp.arange(T, dtype=jnp.int32)
    pos_s = PAST_LEN + jnp.arange(Ts, dtype=jnp.int32)
    xp, xs = x_prompt, x_sample
    ret_p, hg_p, ret_s, hg_s = [], [], [], []
    for l in range(DEPTH):
        weights = (lb_all[l], w_in[l], w_a[l], w_b[l], w_out[l], w_up[l], w_down[l],
                   g_pre_mix[l], g_post_mix[l], g_pre_ffn[l], g_post_ffn[l])
        zr = jnp.zeros((B, RET_HEADS, RET_DK, RET_DV), f32)
        zh = jnp.zeros((B, HG_HEADS, HG_DK, HG_DV), f32)
        xp, sr, sh = _layer(xp, pos_p, zr, zh, *weights, CHUNK, HG_BLOCK)
        ret_p.append(sr)
        hg_p.append(sh)
        xs, sr2, sh2 = _layer(xs, pos_s, state_ret[l].astype(f32), state_hgrn[l].astype(f32),
                              *weights, Ts, Ts)
        ret_s.append(sr2)
        hg_s.append(sh2)
    return (xp, xs,
            jnp.stack(ret_p).astype(x_prompt.dtype), jnp.stack(hg_p).astype(x_prompt.dtype),
            jnp.stack(ret_s).astype(state_ret.dtype), jnp.stack(hg_s).astype(state_hgrn.dtype))
```

```python
import functools

import numpy as np
import jax
import jax.numpy as jnp
from jax import lax
from jax.experimental import pallas as pl
from jax.experimental.pallas import tpu as pltpu

F32 = jnp.float32
BF16 = jnp.bfloat16

D_MODEL = 1024
RET_HEADS = 4
RET_DK = 256
HG_HEADS = 8
HG_DK = 128
CHUNK = 64
PAST_LEN = 2048
ROPE_BASE = 10000.0
EPS = 1e-6

V7X_LANES = 128
V7X_SUBLANES = 8
GROUP = 256
N_GROUPS = 10 * D_MODEL // GROUP
DIAG_LEVEL = 64

PROMPT_TILE = 256
MLP_ROWS = 512
VMEM_LIMIT_MIXER = 50 * 1024 * 1024
VMEM_LIMIT_MLP = 40 * 1024 * 1024


def _dot(a, b):
    return jnp.dot(a, b, preferred_element_type=F32)


def _dot_nt(a, b):
    return lax.dot_general(a, b, (((1,), (1,)), ((), ())), preferred_element_type=F32)


def _dot_tn(a, b):
    return lax.dot_general(a, b, (((0,), (0,)), ((), ())), preferred_element_type=F32)


def _sigmoid(x):
    return 1.0 / (1.0 + jnp.exp(-x))


def _rms(x, g):
    return x * lax.rsqrt(jnp.mean(x * x, axis=-1, keepdims=True) + EPS) * g


def _head_rms(x):
    return x * lax.rsqrt(jnp.mean(x * x, axis=-1, keepdims=True) + EPS)


def _shift_down(z, s):
    n = z.shape[0]
    if s % V7X_SUBLANES == 0:
        return jnp.concatenate([z[n - s:], z[:n - s]], axis=0)
    return pltpu.roll(z, s, axis=0)


def _shift_up(z, s):
    n = z.shape[0]
    if s % V7X_SUBLANES == 0:
        return jnp.concatenate([z[s:], z[:s]], axis=0)
    return pltpu.roll(z, n - s, axis=0)


def _mixer_kernel(x_ref, cos_ref, sin_ref, dmask_ref, xi_ref, zeta_ref, dec_ref, lv_ref,
                  lbl_ref, gpre_ref, gpost_ref, win_ref, wa_ref, wb_ref, wout_ref,
                  sret0_ref, shg0_ref, x1_ref, sret_ref, shg_ref, h_scr, *, tt, layer):
    half = RET_DK // 2

    @pl.when(pl.program_id(1) == 0)
    def _():
        sret_ref[...] = sret0_ref[...]
        shg_ref[...] = shg0_ref[...]

    x = x_ref[0]
    h_scr[...] = _rms(x, gpre_ref[...]).astype(BF16)
    cos = cos_ref[...]
    sin = sin_ref[...]

    def rope(u):
        u1, u2 = u[:, :half], u[:, half:]
        return u1 * cos - u2 * sin, u1 * sin + u2 * cos

    def ret_head(hd, ya):
        h = h_scr[...]
        q1, q2 = rope(_dot(h, win_ref[hd]))
        k1, k2 = rope(_dot(h, win_ref[RET_HEADS + hd]) * (RET_DK ** -0.5))
        vb = _dot(h, win_ref[2 * RET_HEADS + hd]).astype(BF16)
        g = _dot(h, win_ref[3 * RET_HEADS + hd])
        xi = xi_ref[hd]
        zeta = zeta_ref[hd]
        qb = jnp.concatenate([q1, q2], axis=1).astype(BF16)
        kb = jnp.concatenate([k1, k2], axis=1).astype(BF16)
        qx = jnp.concatenate([q1 * xi, q2 * xi], axis=1).astype(BF16)
        kz = jnp.concatenate([k1 * zeta, k2 * zeta], axis=1).astype(BF16)
        s_old = sret_ref[0, hd]
        scores = _dot_nt(qb, kb) * dmask_ref[hd]
        o = _dot(scores.astype(BF16), vb) + _dot(qx, s_old.astype(BF16))
        sret_ref[0, hd] = s_old * dec_ref[hd] + _dot_tn(kz, vb)
        y = (g * _sigmoid(g) * _head_rms(o)).astype(BF16)
        return ya + _dot(y, wa_ref[hd])

    y_a = lax.fori_loop(0, RET_HEADS, ret_head, jnp.zeros((tt, D_MODEL), F32))

    row = lax.broadcasted_iota(jnp.int32, (tt, GROUP), 0)
    lv = lv_ref[...]
    hg_base = 4 * RET_HEADS
    n_pairs = HG_HEADS // 2

    def hg_pair(p, yb):
        h = h_scr[...]
        q = _dot(h, win_ref[hg_base + p])
        fz = _dot(h, win_ref[hg_base + n_pairs + p])
        v = _dot(h, win_ref[hg_base + 2 * n_pairs + p])
        g = _dot(h, win_ref[hg_base + 3 * n_pairs + p])
        lbl = lbl_ref[p]
        lmax = jnp.max(lbl, axis=0, keepdims=True)
        lexp = jnp.exp(lbl - lmax)
        lb = (jnp.sum(lexp[0:layer + 1], axis=0, keepdims=True)
              / jnp.sum(lexp, axis=0, keepdims=True))
        f = lb + (1.0 - lb) * _sigmoid(fz)
        k = 1.0 - f
        b = jnp.log(f)
        s = 1
        while s < tt:
            b = b + jnp.where(row >= s, _shift_down(b, s), 0.0)
            s *= 2
        z = b
        att = [jnp.zeros((tt, tt), F32), jnp.zeros((tt, tt), F32)]
        level, size = 0, 1
        while size < tt:
            odd = (row & size) != 0
            bref = jnp.where(odd, _shift_down(z, size), z)
            e = jnp.exp(jnp.where(odd, b - bref, bref - b))
            w = (jnp.where(odd, q, k) * e).astype(BF16)
            for hh in range(2):
                wh = w[:, hh * HG_DK:(hh + 1) * HG_DK]
                att[hh] = jnp.where(lv == level, _dot_nt(wh, wh), att[hh])
            z = jnp.where(odd, z, _shift_up(z, size))
            level += 1
            size *= 2
        qb = q.astype(BF16)
        kb = k.astype(BF16)
        vb = v.astype(BF16)
        qe = (q * jnp.exp(b)).astype(BF16)
        b_last = b[tt - 1:tt, :]
        kd = (k * jnp.exp(b_last - b)).astype(BF16)
        e_last = jnp.exp(b_last)
        outs = []
        for hh in range(2):
            sl = slice(hh * HG_DK, (hh + 1) * HG_DK)
            a = jnp.where(lv == DIAG_LEVEL, _dot_nt(qb[:, sl], kb[:, sl]), att[hh])
            s_old = shg_ref[0, 2 * p + hh]
            o = _dot(a.astype(BF16), vb[:, sl]) + _dot(qe[:, sl], s_old.astype(BF16))
            scale = jnp.broadcast_to(e_last[:, sl], (HG_DK, HG_DK)).T
            shg_ref[0, 2 * p + hh] = s_old * scale + _dot_tn(kd[:, sl], vb[:, sl])
            outs.append(_head_rms(o))
        y = (g * _sigmoid(g) * jnp.concatenate(outs, axis=1)).astype(BF16)
        return yb + _dot(y, wb_ref[p])

    y_b = lax.fori_loop(0, n_pairs, hg_pair, jnp.zeros((tt, D_MODEL), F32))

    h = h_scr[...]
    m_base = hg_base + 4 * n_pairs
    n_m = D_MODEL // GROUP
    cols = []
    for c in range(n_m):
        sl = slice(c * GROUP, (c + 1) * GROUP)
        ma = _dot(h, win_ref[m_base + c])
        mb = _dot(h, win_ref[m_base + n_m + c])
        cols.append(_sigmoid(ma) * y_a[:, sl] + _sigmoid(mb) * y_b[:, sl])
    mix = jnp.concatenate(cols, axis=1).astype(BF16)
    x1_ref[0] = x + _rms(_dot(mix, wout_ref[...]), gpost_ref[...])


def _mlp_kernel(x_ref, gpre_ref, gpost_ref, wup_ref, wdn_ref, o_ref):
    x = x_ref[...]
    h = _rms(x, gpre_ref[...]).astype(BF16)
    acc = jnp.zeros(x.shape, F32)
    for c in range(wup_ref.shape[0]):
        u = jnp.maximum(_dot(h, wup_ref[c]), 0.0)
        acc = acc + _dot((u * u).astype(BF16), wdn_ref[c])
    o_ref[...] = x + _rms(acc, gpost_ref[...])


def _ret_log_gamma():
    g = 1.0 - np.exp(np.linspace(np.log(1.0 / 32), np.log(1.0 / 512), RET_HEADS))
    return jnp.asarray(np.log(g), dtype=F32)


def _tile_tables(tt, chunk):
    lg = _ret_log_gamma()
    idx = jnp.arange(tt, dtype=F32)
    dist = jnp.abs(idx[:, None] - idx[None, :])
    ch = np.arange(tt) // chunk
    allowed = jnp.asarray(ch[None, :] <= ch[:, None])
    dmask = jnp.where(allowed[None], jnp.exp(dist[None] * lg[:, None, None]), 0.0)
    xi = jnp.exp((idx + 1.0)[None, :] * lg[:, None])
    zeta = jnp.exp((tt - 1.0 - idx)[None, :] * lg[:, None])
    xi = jnp.broadcast_to(xi[:, :, None], (RET_HEADS, tt, V7X_LANES))
    zeta = jnp.broadcast_to(zeta[:, :, None], (RET_HEADS, tt, V7X_LANES))
    dec = jnp.broadcast_to(jnp.exp(tt * lg)[:, None, None], (RET_HEADS, 1, RET_DK))
    t = np.arange(tt)
    x = t[:, None] ^ t[None, :]
    lvl = np.floor(np.log2(np.maximum(x, 1))).astype(np.int32)
    lvl = np.where(t[:, None] > t[None, :], lvl, -1)
    lvl = np.where(t[:, None] == t[None, :], DIAG_LEVEL, lvl).astype(np.int32)
    return dmask, xi, zeta, dec, jnp.asarray(lvl)


def _rope_tables(pos):
    inv = ROPE_BASE ** (-jnp.arange(0, RET_DK, 2, dtype=F32) / RET_DK)
    ang = pos.astype(F32)[:, None] * inv[None, :]
    return jnp.cos(ang), jnp.sin(ang)


def _mixer(x, pos, s_ret0, s_hg0, chunk, tt, layer, weights):
    lbl, gpre, gpost, win, wa, wb, wout = weights
    nb, t_len, d = x.shape
    nt = t_len // tt
    cos, sin = _rope_tables(pos)
    dmask, xi, zeta, dec, lvl = _tile_tables(tt, chunk)
    whole = pl.BlockSpec(memory_space=pltpu.VMEM)
    ret_state = pl.BlockSpec((1, RET_HEADS, RET_DK, RET_DK), lambda b, t: (b, 0, 0, 0))
    hg_state = pl.BlockSpec((1, HG_HEADS, HG_DK, HG_DK), lambda b, t: (b, 0, 0, 0))
    x_spec = pl.BlockSpec((1, tt, d), lambda b, t: (b, t, 0))
    rope_spec = pl.BlockSpec((tt, V7X_LANES), lambda b, t: (t, 0))
    return pl.pallas_call(
        functools.partial(_mixer_kernel, tt=tt, layer=layer),
        grid=(nb, nt),
        in_specs=[x_spec, rope_spec, rope_spec] + [whole] * 12 + [ret_state, hg_state],
        out_specs=[x_spec, ret_state, hg_state],
        out_shape=[jax.ShapeDtypeStruct(x.shape, F32),
                   jax.ShapeDtypeStruct(s_ret0.shape, F32),
                   jax.ShapeDtypeStruct(s_hg0.shape, F32)],
        scratch_shapes=[pltpu.VMEM((tt, d), BF16)],
        compiler_params=pltpu.CompilerParams(
            dimension_semantics=("parallel", "arbitrary"),
            vmem_limit_bytes=VMEM_LIMIT_MIXER),
        name="mixer",
    )(x, cos, sin, dmask, xi, zeta, dec, lvl, lbl, gpre, gpost, win, wa, wb, wout,
      s_ret0, s_hg0)


def _mlp(x, rows, weights):
    gpre, gpost, wup, wdn = weights
    n, d = x.shape
    whole = pl.BlockSpec(memory_space=pltpu.VMEM)
    x_spec = pl.BlockSpec((rows, d), lambda i: (i, 0))
    return pl.pallas_call(
        _mlp_kernel,
        grid=(n // rows,),
        in_specs=[x_spec] + [whole] * 4,
        out_specs=x_spec,
        out_shape=jax.ShapeDtypeStruct(x.shape, F32),
        compiler_params=pltpu.CompilerParams(
            dimension_semantics=("parallel",),
            vmem_limit_bytes=VMEM_LIMIT_MLP),
        name="mlp",
    )(x, gpre, gpost, wup, wdn)


def kernel(x_prompt, x_sample, state_ret, state_hgrn, lb_logits, w_in, w_a, w_b, w_out,
           w_up, w_down, g_pre_mix, g_post_mix, g_pre_ffn, g_post_ffn):
    depth = w_in.shape[0]
    nb, t_len, d = x_prompt.shape
    nbs, ts, _ = x_sample.shape
    d_ff = w_up.shape[-1]
    n_ff = d_ff // d
    xp, xs = x_prompt, x_sample
    ret_p, hg_p, ret_s, hg_s = [], [], [], []
    for l in range(depth):
        win = w_in[l].astype(BF16).reshape(d, N_GROUPS, GROUP).transpose(1, 0, 2)
        wa = w_a[l].astype(BF16).reshape(RET_HEADS, RET_DK, d)
        wb = w_b[l].astype(BF16).reshape(HG_HEADS // 2, GROUP, d)
        wout = w_out[l].astype(BF16)
        wup = w_up[l].astype(BF16).reshape(d, n_ff, d).transpose(1, 0, 2)
        wdn = w_down[l].astype(BF16).reshape(n_ff, d, d)
        lbl = lb_logits.astype(F32).reshape(-1, HG_HEADS // 2, GROUP).transpose(1, 0, 2)
        mix_w = (lbl, g_pre_mix[l][None], g_post_mix[l][None], win, wa, wb, wout)
        mlp_w = (g_pre_ffn[l][None], g_post_ffn[l][None], wup, wdn)

        zr = jnp.zeros((nb, RET_HEADS, RET_DK, RET_DK), F32)
        zh = jnp.zeros((nb, HG_HEADS, HG_DK, HG_DK), F32)
        xp, sr, sh = _mixer(xp, jnp.arange(t_len, dtype=jnp.int32), zr, zh, CHUNK,
                            PROMPT_TILE, l, mix_w)
        xp = _mlp(xp.reshape(nb * t_len, d), MLP_ROWS, mlp_w).reshape(nb, t_len, d)
        ret_p.append(sr)
        hg_p.append(sh)

        xs, sr2, sh2 = _mixer(xs, PAST_LEN + jnp.arange(ts, dtype=jnp.int32),
                              state_ret[l].astype(F32), state_hgrn[l].astype(F32), ts, ts, l, mix_w)
        xs = _mlp(xs.reshape(nbs * ts, d), nbs * ts, mlp_w).reshape(nbs, ts, d)
        ret_s.append(sr2)
        hg_s.append(sh2)
    return (xp, xs,
            jnp.stack(ret_p).astype(x_prompt.dtype), jnp.stack(hg_p).astype(x_prompt.dtype),
            jnp.stack(ret_s).astype(state_ret.dtype), jnp.stack(hg_s).astype(state_hgrn.dtype))
```

```python
import functools

import numpy as np
import jax
import jax.numpy as jnp
from jax import lax
from jax.experimental import pallas as pl
from jax.experimental.pallas import tpu as pltpu

F32 = jnp.float32
BF16 = jnp.bfloat16

D_MODEL = 1024
RET_HEADS = 4
RET_DK = 256
HG_HEADS = 8
HG_DK = 128
CHUNK = 64
PAST_LEN = 2048
ROPE_BASE = 10000.0
EPS = 1e-6

V7X_LANES = 128
V7X_SUBLANES = 8
GROUP = 256
N_GROUPS = 10 * D_MODEL // GROUP
DIAG_LEVEL = 64

PROMPT_TILE = 256
MLP_ROWS = 512
VMEM_LIMIT_MIXER = 50 * 1024 * 1024
VMEM_LIMIT_MLP = 40 * 1024 * 1024


def _dot(a, b):
    return jnp.dot(a, b, preferred_element_type=F32)


def _dot_nt(a, b):
    return lax.dot_general(a, b, (((1,), (1,)), ((), ())), preferred_element_type=F32)


def _dot_tn(a, b):
    return lax.dot_general(a, b, (((0,), (0,)), ((), ())), preferred_element_type=F32)


def _sigmoid(x):
    return 1.0 / (1.0 + jnp.exp(-x))


def _rms(x, g):
    return x * lax.rsqrt(jnp.mean(x * x, axis=-1, keepdims=True) + EPS) * g


def _head_rms(x):
    return x * lax.rsqrt(jnp.mean(x * x, axis=-1, keepdims=True) + EPS)


def _shift_down(z, s):
    n = z.shape[0]
    if s % V7X_SUBLANES == 0:
        return jnp.concatenate([z[n - s:], z[:n - s]], axis=0)
    return pltpu.roll(z, s, axis=0)


def _shift_up(z, s):
    n = z.shape[0]
    if s % V7X_SUBLANES == 0:
        return jnp.concatenate([z[s:], z[:s]], axis=0)
    return pltpu.roll(z, n - s, axis=0)


def _mixer_kernel(x_ref, cos_ref, sin_ref, dmask_ref, xi_ref, zeta_ref, dec_ref, lv_ref,
                  lbl_ref, gpre_ref, gpost_ref, win_ref, wa_ref, wb_ref, wout_ref,
                  sret0_ref, shg0_ref, x1_ref, sret_ref, shg_ref, h_scr, *, tt, layer):
    half = RET_DK // 2
    n_pairs = HG_HEADS // 2
    hg_base = 4 * RET_HEADS
    m_base = hg_base + 4 * n_pairs
    n_m = D_MODEL // GROUP

    @pl.when(pl.program_id(1) == 0)
    def _():
        sret_ref[...] = sret0_ref[...]
        shg_ref[...] = shg0_ref[...]

    h_scr[...] = _rms(x_ref[0], gpre_ref[...]).astype(BF16)

    def proj(g):
        return _dot(h_scr[...], win_ref[g])

    def rope(u):
        cos, sin = cos_ref[...], sin_ref[...]
        u1, u2 = u[:, :half], u[:, half:]
        return u1 * cos - u2 * sin, u1 * sin + u2 * cos

    def ret_unit(hd):
        st = {}

        def project():
            st["q"] = proj(hd)
            st["k"] = proj(RET_HEADS + hd)
            st["vb"] = proj(2 * RET_HEADS + hd).astype(BF16)
            st["g"] = proj(3 * RET_HEADS + hd)

        def scores():
            q1, q2 = rope(st.pop("q"))
            k1, k2 = rope(st.pop("k") * (RET_DK ** -0.5))
            xi = xi_ref[hd]
            zeta = zeta_ref[hd]
            qb = jnp.concatenate([q1, q2], axis=1).astype(BF16)
            kb = jnp.concatenate([k1, k2], axis=1).astype(BF16)
            st["qx"] = jnp.concatenate([q1 * xi, q2 * xi], axis=1).astype(BF16)
            st["kz"] = jnp.concatenate([k1 * zeta, k2 * zeta], axis=1).astype(BF16)
            st["s"] = _dot_nt(qb, kb)

        def attend():
            vb = st.pop("vb")
            s_old = sret_ref[0, hd]
            sc = (st.pop("s") * dmask_ref[hd]).astype(BF16)
            st["o"] = _dot(sc, vb) + _dot(st.pop("qx"), s_old.astype(BF16))
            sret_ref[0, hd] = s_old * dec_ref[hd] + _dot_tn(st.pop("kz"), vb)

        def out():
            g = st.pop("g")
            y = (g * _sigmoid(g) * _head_rms(st.pop("o"))).astype(BF16)
            st["y"] = _dot(y, wa_ref[hd])

        return [project, scores, attend, out], st

    def hg_unit(p):
        st = {}

        def project():
            st["q"] = proj(hg_base + p)
            st["fz"] = proj(hg_base + n_pairs + p)
            st["vb"] = proj(hg_base + 2 * n_pairs + p).astype(BF16)
            st["g"] = proj(hg_base + 3 * n_pairs + p)

        def levels():
            row = lax.broadcasted_iota(jnp.int32, (tt, GROUP), 0)
            lv = lv_ref[...]
            q = st.pop("q")
            lbl = lbl_ref[p]
            lexp = jnp.exp(lbl - jnp.max(lbl, axis=0, keepdims=True))
            lb = (jnp.sum(lexp[0:layer + 1], axis=0, keepdims=True)
                  / jnp.sum(lexp, axis=0, keepdims=True))
            f = lb + (1.0 - lb) * _sigmoid(st.pop("fz"))
            k = 1.0 - f
            b = jnp.log(f)
            s = 1
            while s < tt:
                b = b + jnp.where(row >= s, _shift_down(b, s), 0.0)
                s *= 2
            z = b
            att = [jnp.zeros((tt, tt), F32), jnp.zeros((tt, tt), F32)]
            level, size = 0, 1
            while size < tt:
                odd = (row & size) != 0
                bref = jnp.where(odd, _shift_down(z, size), z)
                e = jnp.exp(jnp.where(odd, b - bref, bref - b))
                w = (jnp.where(odd, q, k) * e).astype(BF16)
                for hh in range(2):
                    wh = w[:, hh * HG_DK:(hh + 1) * HG_DK]
                    att[hh] = jnp.where(lv == level, _dot_nt(wh, wh), att[hh])
                z = jnp.where(odd, z, _shift_up(z, size))
                level += 1
                size *= 2
            qb = q.astype(BF16)
            kb = k.astype(BF16)
            for hh in range(2):
                sl = slice(hh * HG_DK, (hh + 1) * HG_DK)
                a = jnp.where(lv == DIAG_LEVEL, _dot_nt(qb[:, sl], kb[:, sl]), att[hh])
                st["a%d" % hh] = a.astype(BF16)
            b_last = b[tt - 1:tt, :]
            st["qe"] = (q * jnp.exp(b)).astype(BF16)
            st["kd"] = (k * jnp.exp(b_last - b)).astype(BF16)
            st["e_last"] = jnp.exp(b_last)

        def attend():
            vb, qe, kd, e_last = st.pop("vb"), st.pop("qe"), st.pop("kd"), st.pop("e_last")
            outs = []
            for hh in range(2):
                sl = slice(hh * HG_DK, (hh + 1) * HG_DK)
                s_old = shg_ref[0, 2 * p + hh]
                o = _dot(st.pop("a%d" % hh), vb[:, sl]) + _dot(qe[:, sl], s_old.astype(BF16))
                scale = jnp.broadcast_to(e_last[:, sl], (HG_DK, HG_DK)).T
                shg_ref[0, 2 * p + hh] = s_old * scale + _dot_tn(kd[:, sl], vb[:, sl])
                outs.append(_head_rms(o))
            st["o"] = jnp.concatenate(outs, axis=1)

        def out():
            g = st.pop("g")
            y = (g * _sigmoid(g) * st.pop("o")).astype(BF16)
            st["y"] = _dot(y, wb_ref[p])

        return [project, levels, attend, out], st

    def gate_unit(c):
        st = {}

        def project():
            st["ga"] = _sigmoid(proj(m_base + c))
            st["gb"] = _sigmoid(proj(m_base + n_m + c))

        return [project], st

    units = []
    for i in range(RET_HEADS):
        units.append(ret_unit(i))
        units.append(hg_unit(i))
    n_branch = len(units)
    gates = [gate_unit(c) for c in range(n_m)]
    n_stage = 4
    n_tau = n_branch + n_stage - 1
    gate_at = {n_tau - n_m + c: c for c in range(n_m)}
    for tau in range(n_tau):
        for u in range(n_branch):
            s = tau - u
            if 0 <= s < n_stage:
                units[u][0][s]()
        if tau in gate_at:
            gates[gate_at[tau]][0][0]()

    y_a = sum(units[2 * i][1]["y"] for i in range(RET_HEADS))
    y_b = sum(units[2 * i + 1][1]["y"] for i in range(n_pairs))
    cols = []
    for c in range(n_m):
        sl = slice(c * GROUP, (c + 1) * GROUP)
        cols.append(gates[c][1]["ga"] * y_a[:, sl] + gates[c][1]["gb"] * y_b[:, sl])
    mix = jnp.concatenate(cols, axis=1).astype(BF16)
    x1_ref[0] = x_ref[0] + _rms(_dot(mix, wout_ref[...]), gpost_ref[...])


def _mlp_kernel(x_ref, gpre_ref, gpost_ref, wup_ref, wdn_ref, o_ref):
    x = x_ref[...]
    h = _rms(x, gpre_ref[...]).astype(BF16)
    acc = jnp.zeros(x.shape, F32)
    for c in range(wup_ref.shape[0]):
        u = jnp.maximum(_dot(h, wup_ref[c]), 0.0)
        acc = acc + _dot((u * u).astype(BF16), wdn_ref[c])
    o_ref[...] = x + _rms(acc, gpost_ref[...])


def _ret_log_gamma():
    g = 1.0 - np.exp(np.linspace(np.log(1.0 / 32), np.log(1.0 / 512), RET_HEADS))
    return jnp.asarray(np.log(g), dtype=F32)


def _tile_tables(tt, chunk):
    lg = _ret_log_gamma()
    idx = jnp.arange(tt, dtype=F32)
    dist = jnp.abs(idx[:, None] - idx[None, :])
    ch = np.arange(tt) // chunk
    allowed = jnp.asarray(ch[None, :] <= ch[:, None])
    dmask = jnp.where(allowed[None], jnp.exp(dist[None] * lg[:, None, None]), 0.0)
    xi = jnp.exp((idx + 1.0)[None, :] * lg[:, None])
    zeta = jnp.exp((tt - 1.0 - idx)[None, :] * lg[:, None])
    xi = jnp.broadcast_to(xi[:, :, None], (RET_HEADS, tt, V7X_LANES))
    zeta = jnp.broadcast_to(zeta[:, :, None], (RET_HEADS, tt, V7X_LANES))
    dec = jnp.broadcast_to(jnp.exp(tt * lg)[:, None, None], (RET_HEADS, 1, RET_DK))
    t = np.arange(tt)
    x = t[:, None] ^ t[None, :]
    lvl = np.floor(np.log2(np.maximum(x, 1))).astype(np.int32)
    lvl = np.where(t[:, None] > t[None, :], lvl, -1)
    lvl = np.where(t[:, None] == t[None, :], DIAG_LEVEL, lvl).astype(np.int32)
    return dmask, xi, zeta, dec, jnp.asarray(lvl)


def _rope_tables(pos):
    inv = ROPE_BASE ** (-jnp.arange(0, RET_DK, 2, dtype=F32) / RET_DK)
    ang = pos.astype(F32)[:, None] * inv[None, :]
    return jnp.cos(ang), jnp.sin(ang)


def _mixer(x, pos, s_ret0, s_hg0, chunk, tt, layer, weights):
    lbl, gpre, gpost, win, wa, wb, wout = weights
    nb, t_len, d = x.shape
    nt = t_len // tt
    cos, sin = _rope_tables(pos)
    dmask, xi, zeta, dec, lvl = _tile_tables(tt, chunk)
    whole = pl.BlockSpec(memory_space=pltpu.VMEM)
    ret_state = pl.BlockSpec((1, RET_HEADS, RET_DK, RET_DK), lambda b, t: (b, 0, 0, 0))
    hg_state = pl.BlockSpec((1, HG_HEADS, HG_DK, HG_DK), lambda b, t: (b, 0, 0, 0))
    x_spec = pl.BlockSpec((1, tt, d), lambda b, t: (b, t, 0))
    rope_spec = pl.BlockSpec((tt, V7X_LANES), lambda b, t: (t, 0))
    return pl.pallas_call(
        functools.partial(_mixer_kernel, tt=tt, layer=layer),
        grid=(nb, nt),
        in_specs=[x_spec, rope_spec, rope_spec] + [whole] * 12 + [ret_state, hg_state],
        out_specs=[x_spec, ret_state, hg_state],
        out_shape=[jax.ShapeDtypeStruct(x.shape, F32),
                   jax.ShapeDtypeStruct(s_ret0.shape, F32),
                   jax.ShapeDtypeStruct(s_hg0.shape, F32)],
        scratch_shapes=[pltpu.VMEM((tt, d), BF16)],
        compiler_params=pltpu.CompilerParams(
            dimension_semantics=("parallel", "arbitrary"),
            vmem_limit_bytes=VMEM_LIMIT_MIXER),
        name="mixer",
    )(x, cos, sin, dmask, xi, zeta, dec, lvl, lbl, gpre, gpost, win, wa, wb, wout,
      s_ret0, s_hg0)


def _mlp(x, rows, weights):
    gpre, gpost, wup, wdn = weights
    n, d = x.shape
    whole = pl.BlockSpec(memory_space=pltpu.VMEM)
    x_spec = pl.BlockSpec((rows, d), lambda i: (i, 0))
    return pl.pallas_call(
        _mlp_kernel,
        grid=(n // rows,),
        in_specs=[x_spec] + [whole] * 4,
        out_specs=x_spec,
        out_shape=jax.ShapeDtypeStruct(x.shape, F32),
        compiler_params=pltpu.CompilerParams(
            dimension_semantics=("parallel",),
            vmem_limit_bytes=VMEM_LIMIT_MLP),
        name="mlp",
    )(x, gpre, gpost, wup, wdn)


def kernel(x_prompt, x_sample, state_ret, state_hgrn, lb_logits, w_in, w_a, w_b, w_out,
           w_up, w_down, g_pre_mix, g_post_mix, g_pre_ffn, g_post_ffn):
    depth = w_in.shape[0]
    nb, t_len, d = x_prompt.shape
    nbs, ts, _ = x_sample.shape
    d_ff = w_up.shape[-1]
    n_ff = d_ff // d
    xp, xs = x_prompt, x_sample
    ret_p, hg_p, ret_s, hg_s = [], [], [], []
    for l in range(depth):
        win = w_in[l].astype(BF16).reshape(d, N_GROUPS, GROUP).transpose(1, 0, 2)
        wa = w_a[l].astype(BF16).reshape(RET_HEADS, RET_DK, d)
        wb = w_b[l].astype(BF16).reshape(HG_HEADS // 2, GROUP, d)
        wout = w_out[l].astype(BF16)
        wup = w_up[l].astype(BF16).reshape(d, n_ff, d).transpose(1, 0, 2)
        wdn = w_down[l].astype(BF16).reshape(n_ff, d, d)
        lbl = lb_logits.astype(F32).reshape(-1, HG_HEADS // 2, GROUP).transpose(1, 0, 2)
        mix_w = (lbl, g_pre_mix[l][None], g_post_mix[l][None], win, wa, wb, wout)
        mlp_w = (g_pre_ffn[l][None], g_post_ffn[l][None], wup, wdn)

        xs, sr2, sh2 = _mixer(xs, PAST_LEN + jnp.arange(ts, dtype=jnp.int32),
                              state_ret[l].astype(F32), state_hgrn[l].astype(F32), ts, ts, l, mix_w)
        xs = _mlp(xs.reshape(nbs * ts, d), nbs * ts, mlp_w).reshape(nbs, ts, d)
        ret_s.append(sr2)
        hg_s.append(sh2)

        zr = jnp.zeros((nb, RET_HEADS, RET_DK, RET_DK), F32)
        zh = jnp.zeros((nb, HG_HEADS, HG_DK, HG_DK), F32)
        xp, sr, sh = _mixer(xp, jnp.arange(t_len, dtype=jnp.int32), zr, zh, CHUNK,
                            PROMPT_TILE, l, mix_w)
        xp = _mlp(xp.reshape(nb * t_len, d), MLP_ROWS, mlp_w).reshape(nb, t_len, d)
        ret_p.append(sr)
        hg_p.append(sh)
    return (xp, xs,
            jnp.stack(ret_p).astype(x_prompt.dtype), jnp.stack(hg_p).astype(x_prompt.dtype),
            jnp.stack(ret_s).astype(state_ret.dtype), jnp.stack(hg_s).astype(state_hgrn.dtype))
```

```python
import functools

import numpy as np
import jax
import jax.numpy as jnp
from jax import lax
from jax.experimental import pallas as pl
from jax.experimental.pallas import tpu as pltpu

F32 = jnp.float32
BF16 = jnp.bfloat16

D_MODEL = 1024
RET_HEADS = 4
RET_DK = 256
HG_HEADS = 8
HG_DK = 128
CHUNK = 64
PAST_LEN = 2048
ROPE_BASE = 10000.0
EPS = 1e-6

V7X_LANES = 128
V7X_SUBLANES = 8
GROUP = 256
DIAG_LEVEL = 64

PROMPT_TILE = 256
MLP_ROWS = 512
VMEM_LIMIT_MIXER = 50 * 1024 * 1024
VMEM_LIMIT_MLP = 40 * 1024 * 1024


def _dot(a, b):
    return jnp.dot(a, b, preferred_element_type=F32)


def _dot_nt(a, b):
    return lax.dot_general(a, b, (((1,), (1,)), ((), ())), preferred_element_type=F32)


def _dot_tn(a, b):
    return lax.dot_general(a, b, (((0,), (0,)), ((), ())), preferred_element_type=F32)


def _sigmoid(x):
    return 1.0 / (1.0 + jnp.exp(-x))


def _rms(x, g):
    return x * lax.rsqrt(jnp.mean(x * x, axis=-1, keepdims=True) + EPS) * g


def _head_rms(x):
    return x * lax.rsqrt(jnp.mean(x * x, axis=-1, keepdims=True) + EPS)


def _shift_down(z, s):
    n = z.shape[0]
    if s % V7X_SUBLANES == 0:
        return jnp.concatenate([z[n - s:], z[:n - s]], axis=0)
    return pltpu.roll(z, s, axis=0)


def _shift_up(z, s):
    n = z.shape[0]
    if s % V7X_SUBLANES == 0:
        return jnp.concatenate([z[s:], z[:s]], axis=0)
    return pltpu.roll(z, n - s, axis=0)


def _pair_block(tt):
    return V7X_LANES if tt > V7X_LANES and tt % V7X_LANES == 0 else tt


def _mixer_kernel(*refs, tt, layer, zero_init):
    (x_ref, cos_ref, sin_ref, dmask_ref, xi_ref, zeta_ref, dec_ref, lv_ref,
     lbl_ref, gpre_ref, gpost_ref, win_ref, wa_ref, wb_ref, wout_ref) = refs[:15]
    x1_ref, sret_ref, shg_ref, h_scr = refs[-4:]
    half = RET_DK // 2
    n_pairs = HG_HEADS // 2
    hg_base = 4 * RET_HEADS
    m_base = hg_base + 4 * n_pairs
    n_m = D_MODEL // GROUP
    blk = _pair_block(tt)
    nb = tt // blk

    @pl.when(pl.program_id(1) == 0)
    def _():
        if zero_init:
            sret_ref[...] = jnp.zeros(sret_ref.shape, F32)
            shg_ref[...] = jnp.zeros(shg_ref.shape, F32)
        else:
            sret0_ref, shg0_ref = refs[15:17]
            sret_ref[...] = sret0_ref[...]
            shg_ref[...] = shg0_ref[...]

    h_scr[...] = _rms(x_ref[0], gpre_ref[...]).astype(BF16)

    def proj(g):
        return _dot(h_scr[...], win_ref[:, g * GROUP:(g + 1) * GROUP])

    def rope(u):
        cos, sin = cos_ref[...], sin_ref[...]
        u1, u2 = u[:, :half], u[:, half:]
        return u1 * cos - u2 * sin, u1 * sin + u2 * cos

    def ret_unit(hd):
        st = {}

        def project():
            st["q"] = proj(hd)
            st["k"] = proj(RET_HEADS + hd)
            st["vb"] = proj(2 * RET_HEADS + hd).astype(BF16)
            st["g"] = proj(3 * RET_HEADS + hd)

        def scores():
            q1, q2 = rope(st.pop("q"))
            k1, k2 = rope(st.pop("k") * (RET_DK ** -0.5))
            xi = xi_ref[hd]
            zeta = zeta_ref[hd]
            qb = jnp.concatenate([q1, q2], axis=1).astype(BF16)
            kb = jnp.concatenate([k1, k2], axis=1).astype(BF16)
            st["qx"] = jnp.concatenate([q1 * xi, q2 * xi], axis=1).astype(BF16)
            st["kz"] = jnp.concatenate([k1 * zeta, k2 * zeta], axis=1).astype(BF16)
            st["s"] = _dot_nt(qb, kb)

        def attend():
            vb = st.pop("vb")
            s_old = sret_ref[0, hd]
            sc = (st.pop("s") * dmask_ref[hd]).astype(BF16)
            st["o"] = _dot(sc, vb) + _dot(st.pop("qx"), s_old.astype(BF16))
            sret_ref[0, hd] = s_old * dec_ref[hd] + _dot_tn(st.pop("kz"), vb)

        def out():
            g = st.pop("g")
            y = (g * _sigmoid(g) * _head_rms(st.pop("o"))).astype(BF16)
            st["y"] = _dot(y, wa_ref[hd * RET_DK:(hd + 1) * RET_DK, :])

        return [project, scores, attend, out], st

    def hg_unit(p):
        st = {}

        def project():
            st["q"] = proj(hg_base + p)
            st["fz"] = proj(hg_base + n_pairs + p)
            st["vb"] = proj(hg_base + 2 * n_pairs + p).astype(BF16)
            st["g"] = proj(hg_base + 3 * n_pairs + p)

        def levels():
            row = lax.broadcasted_iota(jnp.int32, (tt, GROUP), 0)
            lvb = lv_ref[...]
            q = st.pop("q")
            lbl = lbl_ref[:, p * GROUP:(p + 1) * GROUP]
            lexp = jnp.exp(lbl - jnp.max(lbl, axis=0, keepdims=True))
            lb = (jnp.sum(lexp[0:layer + 1], axis=0, keepdims=True)
                  / jnp.sum(lexp, axis=0, keepdims=True))
            f = lb + (1.0 - lb) * _sigmoid(st.pop("fz"))
            k = 1.0 - f
            b = jnp.log(f)
            s = 1
            while s < tt:
                b = b + jnp.where(row >= s, _shift_down(b, s), 0.0)
                s *= 2
            z = b
            diag_blk = [[jnp.zeros((blk, blk), F32) for _ in range(nb)] for _ in range(2)]
            off_blk = [{} for _ in range(2)]
            level, size = 0, 1
            while size < tt:
                odd = (row & size) != 0
                bref = jnp.where(odd, _shift_down(z, size), z)
                e = jnp.exp(jnp.where(odd, b - bref, bref - b))
                w = (jnp.where(odd, q, k) * e).astype(BF16)
                for hh in range(2):
                    wh = w[:, hh * HG_DK:(hh + 1) * HG_DK]
                    if size < blk:
                        for i in range(nb):
                            wi = wh[i * blk:(i + 1) * blk]
                            diag_blk[hh][i] = jnp.where(lvb == level, _dot_nt(wi, wi),
                                                        diag_blk[hh][i])
                    else:
                        for i in range(nb):
                            for j in range(i):
                                if ((i * blk) ^ (j * blk)).bit_length() - 1 == level:
                                    off_blk[hh][(i, j)] = _dot_nt(wh[i * blk:(i + 1) * blk],
                                                                  wh[j * blk:(j + 1) * blk])
                z = jnp.where(odd, z, _shift_up(z, size))
                level += 1
                size *= 2
            qb = q.astype(BF16)
            kb = k.astype(BF16)
            for hh in range(2):
                sl = slice(hh * HG_DK, (hh + 1) * HG_DK)
                for i in range(nb):
                    rows = slice(i * blk, (i + 1) * blk)
                    d = jnp.where(lvb == DIAG_LEVEL, _dot_nt(qb[rows, sl], kb[rows, sl]),
                                  diag_blk[hh][i])
                    parts = [off_blk[hh][(i, j)] for j in range(i)] + [d]
                    st["a%d_%d" % (hh, i)] = jnp.concatenate(parts, axis=1).astype(BF16)
            b_last = b[tt - 1:tt, :]
            st["qe"] = (q * jnp.exp(b)).astype(BF16)
            st["kd"] = (k * jnp.exp(b_last - b)).astype(BF16)
            st["e_last"] = jnp.exp(b_last)

        def attend():
            vb, qe, kd, e_last = st.pop("vb"), st.pop("qe"), st.pop("kd"), st.pop("e_last")
            outs = []
            for hh in range(2):
                sl = slice(hh * HG_DK, (hh + 1) * HG_DK)
                s_old = shg_ref[0, 2 * p + hh]
                intra = [_dot(st.pop("a%d_%d" % (hh, i)), vb[:(i + 1) * blk, sl])
                         for i in range(nb)]
                o = jnp.concatenate(intra, axis=0) + _dot(qe[:, sl], s_old.astype(BF16))
                scale = jnp.broadcast_to(e_last[:, sl], (HG_DK, HG_DK)).T
                shg_ref[0, 2 * p + hh] = s_old * scale + _dot_tn(kd[:, sl], vb[:, sl])
                outs.append(_head_rms(o))
            st["o"] = jnp.concatenate(outs, axis=1)

        def out():
            g = st.pop("g")
            y = (g * _sigmoid(g) * st.pop("o")).astype(BF16)
            st["y"] = _dot(y, wb_ref[p * GROUP:(p + 1) * GROUP, :])

        return [project, levels, attend, out], st

    def gate_unit(c):
        st = {}

        def project():
            st["ga"] = _sigmoid(proj(m_base + c))
            st["gb"] = _sigmoid(proj(m_base + n_m + c))

        return [project], st

    units = []
    for i in range(RET_HEADS):
        units.append(ret_unit(i))
        units.append(hg_unit(i))
    n_branch = len(units)
    gates = [gate_unit(c) for c in range(n_m)]
    n_stage = 4
    n_tau = n_branch + n_stage - 1
    gate_at = {n_tau - n_m + c: c for c in range(n_m)}
    for tau in range(n_tau):
        for u in range(n_branch):
            s = tau - u
            if 0 <= s < n_stage:
                units[u][0][s]()
        if tau in gate_at:
            gates[gate_at[tau]][0][0]()

    y_a = sum(units[2 * i][1]["y"] for i in range(RET_HEADS))
    y_b = sum(units[2 * i + 1][1]["y"] for i in range(n_pairs))
    cols = []
    for c in range(n_m):
        sl = slice(c * GROUP, (c + 1) * GROUP)
        cols.append(gates[c][1]["ga"] * y_a[:, sl] + gates[c][1]["gb"] * y_b[:, sl])
    mix = jnp.concatenate(cols, axis=1).astype(BF16)
    x1_ref[0] = x_ref[0] + _rms(_dot(mix, wout_ref[...]), gpost_ref[...])


def _mlp_kernel(x_ref, gpre_ref, gpost_ref, wup_ref, wdn_ref, o_ref):
    x = x_ref[...]
    h = _rms(x, gpre_ref[...]).astype(BF16)
    acc = jnp.zeros(x.shape, F32)
    d = x.shape[1]
    for c in range(wup_ref.shape[1] // d):
        u = jnp.maximum(_dot(h, wup_ref[:, c * d:(c + 1) * d]), 0.0)
        acc = acc + _dot((u * u).astype(BF16), wdn_ref[c * d:(c + 1) * d, :])
    o_ref[...] = x + _rms(acc, gpost_ref[...])


def _ret_log_gamma():
    g = 1.0 - np.exp(np.linspace(np.log(1.0 / 32), np.log(1.0 / 512), RET_HEADS))
    return jnp.asarray(np.log(g), dtype=F32)


def _tile_tables(tt, chunk):
    lg = _ret_log_gamma()
    idx = jnp.arange(tt, dtype=F32)
    dist = jnp.abs(idx[:, None] - idx[None, :])
    ch = np.arange(tt) // chunk
    allowed = jnp.asarray(ch[None, :] <= ch[:, None])
    dmask = jnp.where(allowed[None], jnp.exp(dist[None] * lg[:, None, None]), 0.0)
    xi = jnp.exp((idx + 1.0)[None, :] * lg[:, None])
    zeta = jnp.exp((tt - 1.0 - idx)[None, :] * lg[:, None])
    xi = jnp.broadcast_to(xi[:, :, None], (RET_HEADS, tt, V7X_LANES))
    zeta = jnp.broadcast_to(zeta[:, :, None], (RET_HEADS, tt, V7X_LANES))
    dec = jnp.broadcast_to(jnp.exp(tt * lg)[:, None, None], (RET_HEADS, 1, RET_DK))
    t = np.arange(_pair_block(tt))
    x = t[:, None] ^ t[None, :]
    lvl = np.floor(np.log2(np.maximum(x, 1))).astype(np.int32)
    lvl = np.where(t[:, None] > t[None, :], lvl, -1)
    lvl = np.where(t[:, None] == t[None, :], DIAG_LEVEL, lvl).astype(np.int32)
    return dmask, xi, zeta, dec, jnp.asarray(lvl)


def _rope_tables(pos):
    inv = ROPE_BASE ** (-jnp.arange(0, RET_DK, 2, dtype=F32) / RET_DK)
    ang = pos.astype(F32)[:, None] * inv[None, :]
    return jnp.cos(ang), jnp.sin(ang)


def _mixer(x, pos, states, chunk, tt, layer, weights):
    lbl, gpre, gpost, win, wa, wb, wout = weights
    nb, t_len, d = x.shape
    nt = t_len // tt
    cos, sin = _rope_tables(pos)
    dmask, xi, zeta, dec, lvl = _tile_tables(tt, chunk)
    whole = pl.BlockSpec(memory_space=pltpu.VMEM)
    ret_shape = (nb, RET_HEADS, RET_DK, RET_DK)
    hg_shape = (nb, HG_HEADS, HG_DK, HG_DK)
    ret_state = pl.BlockSpec((1,) + ret_shape[1:], lambda b, t: (b, 0, 0, 0))
    hg_state = pl.BlockSpec((1,) + hg_shape[1:], lambda b, t: (b, 0, 0, 0))
    x_spec = pl.BlockSpec((1, tt, d), lambda b, t: (b, t, 0))
    rope_spec = pl.BlockSpec((tt, V7X_LANES), lambda b, t: (t, 0))
    state_specs = [] if states is None else [ret_state, hg_state]
    state_args = () if states is None else tuple(states)
    return pl.pallas_call(
        functools.partial(_mixer_kernel, tt=tt, layer=layer, zero_init=states is None),
        grid=(nb, nt),
        in_specs=[x_spec, rope_spec, rope_spec] + [whole] * 12 + state_specs,
        out_specs=[x_spec, ret_state, hg_state],
        out_shape=[jax.ShapeDtypeStruct(x.shape, F32),
                   jax.ShapeDtypeStruct(ret_shape, F32),
                   jax.ShapeDtypeStruct(hg_shape, F32)],
        scratch_shapes=[pltpu.VMEM((tt, d), BF16)],
        compiler_params=pltpu.CompilerParams(
            dimension_semantics=("parallel", "arbitrary"),
            vmem_limit_bytes=VMEM_LIMIT_MIXER),
        name="mixer",
    )(x, cos, sin, dmask, xi, zeta, dec, lvl, lbl, gpre, gpost, win, wa, wb, wout,
      *state_args)


def _mlp(x, rows, weights):
    gpre, gpost, wup, wdn = weights
    n, d = x.shape
    whole = pl.BlockSpec(memory_space=pltpu.VMEM)
    x_spec = pl.BlockSpec((rows, d), lambda i: (i, 0))
    return pl.pallas_call(
        _mlp_kernel,
        grid=(n // rows,),
        in_specs=[x_spec] + [whole] * 4,
        out_specs=x_spec,
        out_shape=jax.ShapeDtypeStruct(x.shape, F32),
        compiler_params=pltpu.CompilerParams(
            dimension_semantics=("parallel",),
            vmem_limit_bytes=VMEM_LIMIT_MLP),
        name="mlp",
    )(x, gpre, gpost, wup, wdn)


def kernel(x_prompt, x_sample, state_ret, state_hgrn, lb_logits, w_in, w_a, w_b, w_out,
           w_up, w_down, g_pre_mix, g_post_mix, g_pre_ffn, g_post_ffn):
    depth = w_in.shape[0]
    nb, t_len, d = x_prompt.shape
    nbs, ts, _ = x_sample.shape
    xp, xs = x_prompt, x_sample
    ret_p, hg_p, ret_s, hg_s = [], [], [], []
    for l in range(depth):
        mix_w = (lb_logits.astype(F32), g_pre_mix[l][None], g_post_mix[l][None],
                 w_in[l].astype(BF16), w_a[l].astype(BF16), w_b[l].astype(BF16),
                 w_out[l].astype(BF16))
        mlp_w = (g_pre_ffn[l][None], g_post_ffn[l][None],
                 w_up[l].astype(BF16), w_down[l].astype(BF16))

        xs, sr2, sh2 = _mixer(xs, PAST_LEN + jnp.arange(ts, dtype=jnp.int32),
                              (state_ret[l].astype(F32), state_hgrn[l].astype(F32)),
                              ts, ts, l, mix_w)
        xs = _mlp(xs.reshape(nbs * ts, d), nbs * ts, mlp_w).reshape(nbs, ts, d)
        ret_s.append(sr2)
        hg_s.append(sh2)

        xp, sr, sh = _mixer(xp, jnp.arange(t_len, dtype=jnp.int32), None, CHUNK,
                            PROMPT_TILE, l, mix_w)
        xp = _mlp(xp.reshape(nb * t_len, d), MLP_ROWS, mlp_w).reshape(nb, t_len, d)
        ret_p.append(sr)
        hg_p.append(sh)
    return (xp, xs,
            jnp.stack(ret_p).astype(x_prompt.dtype), jnp.stack(hg_p).astype(x_prompt.dtype),
            jnp.stack(ret_s).astype(state_ret.dtype), jnp.stack(hg_s).astype(state_hgrn.dtype))
```

```python
import functools

import numpy as np
import jax
import jax.numpy as jnp
from jax import lax
from jax.experimental import pallas as pl
from jax.experimental.pallas import tpu as pltpu

F32 = jnp.float32
BF16 = jnp.bfloat16

D_MODEL = 1024
RET_HEADS = 4
RET_DK = 256
HG_HEADS = 8
HG_DK = 128
CHUNK = 64
PAST_LEN = 2048
ROPE_BASE = 10000.0
EPS = 1e-6

V7X_LANES = 128
V7X_SUBLANES = 8
GROUP = 256
DIAG_LEVEL = 64
LOG2E = 1.4426950408889634

PROMPT_TILE = 256
MLP_ROWS = 512
VMEM_LIMIT_MIXER = 50 * 1024 * 1024
VMEM_LIMIT_MLP = 40 * 1024 * 1024


def _dot(a, b):
    return jnp.dot(a, b, preferred_element_type=F32)


def _dot_nt(a, b):
    return lax.dot_general(a, b, (((1,), (1,)), ((), ())), preferred_element_type=F32)


def _dot_tn(a, b):
    return lax.dot_general(a, b, (((0,), (0,)), ((), ())), preferred_element_type=F32)


def _transposed_bf16(a):
    return a.T.astype(BF16)


def _sigmoid(x):
    return 0.5 * jnp.tanh(0.5 * x) + 0.5


def _silu(x):
    hx = 0.5 * x
    return hx * jnp.tanh(hx) + hx


def _rms(x, g):
    return x * lax.rsqrt(jnp.mean(x * x, axis=-1, keepdims=True) + EPS) * g


def _head_rms(x):
    return x * lax.rsqrt(jnp.mean(x * x, axis=-1, keepdims=True) + EPS)


def _shift_down(z, s):
    n = z.shape[0]
    if s % V7X_SUBLANES == 0:
        return jnp.concatenate([z[n - s:], z[:n - s]], axis=0)
    return pltpu.roll(z, s, axis=0)


def _shift_up(z, s):
    n = z.shape[0]
    if s % V7X_SUBLANES == 0:
        return jnp.concatenate([z[s:], z[:s]], axis=0)
    return pltpu.roll(z, n - s, axis=0)


def _row_blocks(a, size):
    return [a[i * size:(i + 1) * size] for i in range(a.shape[0] // size)]


def _pair_block(tt):
    return V7X_LANES if tt > V7X_LANES and tt % V7X_LANES == 0 else tt


def _mixer_kernel(*refs, tt, layer, zero_init):
    (x_ref, cos_ref, sin_ref, dmask_ref, xi_ref, zeta_ref, dec_ref, lv_ref,
     lbl_ref, gpre_ref, gpost_ref, win_ref, wa_ref, wb_ref, wout_ref) = refs[:15]
    x1_ref, sret_ref, shg_ref, h_scr = refs[-4:]
    half = RET_DK // 2
    n_pairs = HG_HEADS // 2
    hg_base = 4 * RET_HEADS
    m_base = hg_base + 4 * n_pairs
    n_m = D_MODEL // GROUP
    blk = _pair_block(tt)
    nb = tt // blk

    @pl.when(pl.program_id(1) == 0)
    def _():
        if zero_init:
            sret_ref[...] = jnp.zeros(sret_ref.shape, F32)
            shg_ref[...] = jnp.zeros(shg_ref.shape, F32)
        else:
            sret0_ref, shg0_ref = refs[15:17]
            sret_ref[...] = sret0_ref[...]
            shg_ref[...] = shg0_ref[...]

    h_scr[...] = _rms(x_ref[0], gpre_ref[...]).astype(BF16)

    tile_row = lax.broadcasted_iota(jnp.int32, (tt, GROUP), 0)
    sgn_log2e = []
    size = 1
    while size < tt:
        sgn_log2e.append(jnp.where((tile_row & size) != 0, LOG2E, -LOG2E))
        size *= 2

    def proj(g):
        return _dot(h_scr[...], win_ref[g])

    def rope(u):
        cos, sin = cos_ref[...], sin_ref[...]
        u1, u2 = u[:, :half], u[:, half:]
        return u1 * cos - u2 * sin, u1 * sin + u2 * cos

    def ret_unit(hd):
        st = {}

        def p_q():
            st["q"] = proj(hd)

        def p_k():
            st["k"] = proj(RET_HEADS + hd)

        def p_v():
            st["vb"] = proj(2 * RET_HEADS + hd).astype(BF16)

        def p_g():
            st["g"] = proj(3 * RET_HEADS + hd)

        def scores():
            q1, q2 = rope(st.pop("q"))
            k1, k2 = rope(st.pop("k") * (RET_DK ** -0.5))
            xi = xi_ref[hd]
            zeta = zeta_ref[hd]
            qb = jnp.concatenate([q1, q2], axis=1).astype(BF16)
            kb = jnp.concatenate([k1, k2], axis=1).astype(BF16)
            st["qx"] = jnp.concatenate([q1 * xi, q2 * xi], axis=1).astype(BF16)
            st["kz"] = jnp.concatenate([k1 * zeta, k2 * zeta], axis=1).astype(BF16)
            st["s"] = _dot_nt(qb, kb)

        def a_intra():
            sc = (st.pop("s") * dmask_ref[hd]).astype(BF16)
            st["o"] = _dot(sc, st["vb"])

        def a_inter():
            st["s_old"] = sret_ref[0, hd]
            st["o"] = st["o"] + _dot(st.pop("qx"), st["s_old"].astype(BF16))

        def a_state():
            sret_ref[0, hd] = (st.pop("s_old") * dec_ref[hd]
                               + _dot_tn(st.pop("kz"), st.pop("vb")))

        def out():
            g = st.pop("g")
            y = (_silu(g) * _head_rms(st.pop("o"))).astype(BF16)
            st["y"] = _dot(y, wa_ref[hd * RET_DK:(hd + 1) * RET_DK, :])

        return [[p_q, p_k, p_v, p_g], [scores], [a_intra, a_inter, a_state], [out]], st

    def hg_unit(p):
        st = {}

        def p_q():
            st["q"] = proj(hg_base + p)

        def p_f():
            st["fz"] = proj(hg_base + n_pairs + p)

        def p_v():
            st["vb"] = proj(hg_base + 2 * n_pairs + p).astype(BF16)

        def p_g():
            st["g"] = proj(hg_base + 3 * n_pairs + p)

        def decay():
            row = lax.broadcasted_iota(jnp.int32, (tt, GROUP), 0)
            lbl = lbl_ref[:, p * GROUP:(p + 1) * GROUP]
            lexp = jnp.exp(lbl - jnp.max(lbl, axis=0, keepdims=True))
            lb = (jnp.sum(lexp[0:layer + 1], axis=0, keepdims=True)
                  / jnp.sum(lexp, axis=0, keepdims=True))
            f = lb + (1.0 - lb) / (1.0 + jnp.exp(-st.pop("fz")))
            k = 1.0 - f
            b = jnp.log(f)
            s = 1
            while s < tt:
                if s % V7X_SUBLANES == 0:
                    b = b + jnp.concatenate([jnp.zeros((s, GROUP), F32), b[:tt - s]], axis=0)
                else:
                    b = b + jnp.where(row >= s, _shift_down(b, s), 0.0)
                s *= 2
            st["k"], st["b"], st["z"] = k, b, b
            st["diag"] = [[jnp.zeros((blk, blk), F32) for _ in range(nb)] for _ in range(2)]
            st["off"] = [{} for _ in range(2)]

        def level_step(level):
            size = 1 << level

            def step():
                row = lax.broadcasted_iota(jnp.int32, (tt, GROUP), 0)
                lvb = lv_ref[...]
                q, k, b, z = st["q"], st["k"], st["b"], st["z"]
                diag_blk, off_blk = st["diag"], st["off"]
                if size % V7X_SUBLANES == 0:
                    n_pair = tt // (2 * size)
                    zb, qs, ks = (_row_blocks(a, size) for a in (z, q, k))
                    bref = jnp.concatenate(
                        [zb[2 * g] for g in range(n_pair) for _ in range(2)], axis=0)
                    qk = jnp.concatenate(
                        [blk for g in range(n_pair) for blk in (ks[2 * g], qs[2 * g + 1])], axis=0)
                    z_next = jnp.concatenate(
                        [zb[2 * g + 1] for g in range(n_pair) for _ in range(2)], axis=0)
                else:
                    odd = (row & size) != 0
                    bref = jnp.where(odd, _shift_down(z, size), z)
                    qk = jnp.where(odd, q, k)
                    z_next = jnp.where(odd, z, _shift_up(z, size))
                e = jnp.exp2((b - bref) * sgn_log2e[level])
                wf = qk * e
                w = wf.astype(BF16)
                for hh in range(2):
                    sl = slice(hh * HG_DK, (hh + 1) * HG_DK)
                    if size < blk:
                        for i in range(nb):
                            rows = slice(i * blk, (i + 1) * blk)
                            diag_blk[hh][i] = jnp.where(
                                lvb == level, _dot(w[rows, sl], _transposed_bf16(wf[rows, sl])),
                                diag_blk[hh][i])
                    else:
                        for i in range(nb):
                            for j in range(i):
                                if ((i * blk) ^ (j * blk)).bit_length() - 1 == level:
                                    off_blk[hh][(i, j)] = _dot(
                                        w[i * blk:(i + 1) * blk, sl],
                                        _transposed_bf16(wf[j * blk:(j + 1) * blk, sl]))
                st["z"] = z_next

            return step

        def assemble():
            lvb = lv_ref[...]
            q, k, b = st.pop("q"), st.pop("k"), st.pop("b")
            diag_blk, off_blk = st.pop("diag"), st.pop("off")
            st.pop("z")
            qb = q.astype(BF16)
            for hh in range(2):
                sl = slice(hh * HG_DK, (hh + 1) * HG_DK)
                for i in range(nb):
                    rows = slice(i * blk, (i + 1) * blk)
                    d = jnp.where(lvb == DIAG_LEVEL,
                                  _dot(qb[rows, sl], _transposed_bf16(k[rows, sl])),
                                  diag_blk[hh][i])
                    parts = [off_blk[hh][(i, j)] for j in range(i)] + [d]
                    st["a%d_%d" % (hh, i)] = jnp.concatenate(parts, axis=1).astype(BF16)
            b_last = b[tt - 1:tt, :]
            st["qe"] = (q * jnp.exp(b)).astype(BF16)
            st["kd"] = (k * jnp.exp(b_last - b)).astype(BF16)
            st["e_last"] = jnp.exp(b_last)
            st["outs"] = []

        def attend_step(hh):
            sl = slice(hh * HG_DK, (hh + 1) * HG_DK)

            def read():
                st["s_old"] = shg_ref[0, 2 * p + hh]
                intra = [_dot(st.pop("a%d_%d" % (hh, i)), st["vb"][:(i + 1) * blk, sl])
                         for i in range(nb)]
                o = (jnp.concatenate(intra, axis=0)
                     + _dot(st["qe"][:, sl], st["s_old"].astype(BF16)))
                st["outs"].append(_head_rms(o))

            def update():
                scale = jnp.broadcast_to(st["e_last"][:, sl], (HG_DK, HG_DK)).T
                shg_ref[0, 2 * p + hh] = (st.pop("s_old") * scale
                                          + _dot_tn(st["kd"][:, sl], st["vb"][:, sl]))

            return [read, update]

        def out():
            for name in ("vb", "qe", "kd", "e_last"):
                st.pop(name)
            y = (_silu(st.pop("g")) * jnp.concatenate(st.pop("outs"), axis=1)).astype(BF16)
            st["y"] = _dot(y, wb_ref[p * GROUP:(p + 1) * GROUP, :])

        n_level = tt.bit_length() - 1
        stages = [[p_q, p_f, p_v, p_g],
                  [decay] + [level_step(l) for l in range(n_level)] + [assemble],
                  attend_step(0) + attend_step(1),
                  [out]]
        return stages, st

    def gate_unit(c):
        st = {}

        def p_a():
            st["ga"] = _sigmoid(proj(m_base + c))

        def p_b():
            st["gb"] = _sigmoid(proj(m_base + n_m + c))

        return [[p_a, p_b]], st

    def emit_interleaved(stages):
        keyed = []
        for si, steps in enumerate(stages):
            for j, fn in enumerate(steps):
                keyed.append(((j + 0.5) / len(steps), si, fn))
        for _, _, fn in sorted(keyed, key=lambda item: item[:2]):
            fn()

    units = []
    for i in range(RET_HEADS):
        units.append(ret_unit(i))
        units.append(hg_unit(i))
    n_branch = len(units)
    gates = [gate_unit(c) for c in range(n_m)]
    n_stage = 4
    n_tau = n_branch + n_stage - 1
    gate_at = {n_tau - n_m + c: c for c in range(n_m)}
    for tau in range(n_tau):
        active = [units[u][0][tau - u] for u in range(n_branch) if 0 <= tau - u < n_stage]
        if tau in gate_at:
            active.append(gates[gate_at[tau]][0][0])
        emit_interleaved(active)

    y_a = sum(units[2 * i][1]["y"] for i in range(RET_HEADS))
    y_b = sum(units[2 * i + 1][1]["y"] for i in range(n_pairs))
    cols = []
    for c in range(n_m):
        sl = slice(c * GROUP, (c + 1) * GROUP)
        cols.append(gates[c][1]["ga"] * y_a[:, sl] + gates[c][1]["gb"] * y_b[:, sl])
    mix = jnp.concatenate(cols, axis=1).astype(BF16)
    x1_ref[0] = x_ref[0] + _rms(_dot(mix, wout_ref[...]), gpost_ref[...])


def _mlp_kernel(x_ref, gpre_ref, gpost_ref, wup_ref, wdn_ref, o_ref):
    x = x_ref[...]
    h = _rms(x, gpre_ref[...]).astype(BF16)
    acc = jnp.zeros(x.shape, F32)
    d = x.shape[1]
    for c in range(wup_ref.shape[1] // d):
        u = jnp.maximum(_dot(h, wup_ref[:, c * d:(c + 1) * d]), 0.0)
        acc = acc + _dot((u * u).astype(BF16), wdn_ref[c * d:(c + 1) * d, :])
    o_ref[...] = x + _rms(acc, gpost_ref[...])


def _ret_log_gamma():
    g = 1.0 - np.exp(np.linspace(np.log(1.0 / 32), np.log(1.0 / 512), RET_HEADS))
    return jnp.asarray(np.log(g), dtype=F32)


def _tile_tables(tt, chunk):
    lg = _ret_log_gamma()
    idx = jnp.arange(tt, dtype=F32)
    dist = jnp.abs(idx[:, None] - idx[None, :])
    ch = np.arange(tt) // chunk
    allowed = jnp.asarray(ch[None, :] <= ch[:, None])
    dmask = jnp.where(allowed[None], jnp.exp(dist[None] * lg[:, None, None]), 0.0)
    xi = jnp.exp((idx + 1.0)[None, :] * lg[:, None])
    zeta = jnp.exp((tt - 1.0 - idx)[None, :] * lg[:, None])
    xi = jnp.broadcast_to(xi[:, :, None], (RET_HEADS, tt, V7X_LANES))
    zeta = jnp.broadcast_to(zeta[:, :, None], (RET_HEADS, tt, V7X_LANES))
    dec = jnp.broadcast_to(jnp.exp(tt * lg)[:, None, None], (RET_HEADS, 1, RET_DK))
    t = np.arange(_pair_block(tt))
    x = t[:, None] ^ t[None, :]
    lvl = np.floor(np.log2(np.maximum(x, 1))).astype(np.int32)
    lvl = np.where(t[:, None] > t[None, :], lvl, -1)
    lvl = np.where(t[:, None] == t[None, :], DIAG_LEVEL, lvl).astype(np.int32)
    return dmask, xi, zeta, dec, jnp.asarray(lvl)


def _rope_tables(pos):
    inv = ROPE_BASE ** (-jnp.arange(0, RET_DK, 2, dtype=F32) / RET_DK)
    ang = pos.astype(F32)[:, None] * inv[None, :]
    return jnp.cos(ang), jnp.sin(ang)


def _mixer(x, pos, states, chunk, tt, layer, weights):
    lbl, gpre, gpost, win, wa, wb, wout = weights
    nb, t_len, d = x.shape
    nt = t_len // tt
    cos, sin = _rope_tables(pos)
    dmask, xi, zeta, dec, lvl = _tile_tables(tt, chunk)
    whole = pl.BlockSpec(memory_space=pltpu.VMEM)
    ret_shape = (nb, RET_HEADS, RET_DK, RET_DK)
    hg_shape = (nb, HG_HEADS, HG_DK, HG_DK)
    ret_state = pl.BlockSpec((1,) + ret_shape[1:], lambda b, t: (b, 0, 0, 0))
    hg_state = pl.BlockSpec((1,) + hg_shape[1:], lambda b, t: (b, 0, 0, 0))
    x_spec = pl.BlockSpec((1, tt, d), lambda b, t: (b, t, 0))
    rope_spec = pl.BlockSpec((tt, V7X_LANES), lambda b, t: (t, 0))
    state_specs = [] if states is None else [ret_state, hg_state]
    state_args = () if states is None else tuple(states)
    return pl.pallas_call(
        functools.partial(_mixer_kernel, tt=tt, layer=layer, zero_init=states is None),
        grid=(nb, nt),
        in_specs=[x_spec, rope_spec, rope_spec] + [whole] * 12 + state_specs,
        out_specs=[x_spec, ret_state, hg_state],
        out_shape=[jax.ShapeDtypeStruct(x.shape, F32),
                   jax.ShapeDtypeStruct(ret_shape, F32),
                   jax.ShapeDtypeStruct(hg_shape, F32)],
        scratch_shapes=[pltpu.VMEM((tt, d), BF16)],
        compiler_params=pltpu.CompilerParams(
            dimension_semantics=("parallel", "arbitrary"),
            vmem_limit_bytes=VMEM_LIMIT_MIXER),
        name="mixer",
    )(x, cos, sin, dmask, xi, zeta, dec, lvl, lbl, gpre, gpost, win, wa, wb, wout,
      *state_args)


def _mlp(x, rows, weights):
    gpre, gpost, wup, wdn = weights
    n, d = x.shape
    whole = pl.BlockSpec(memory_space=pltpu.VMEM)
    x_spec = pl.BlockSpec((rows, d), lambda i: (i, 0))
    return pl.pallas_call(
        _mlp_kernel,
        grid=(n // rows,),
        in_specs=[x_spec] + [whole] * 4,
        out_specs=x_spec,
        out_shape=jax.ShapeDtypeStruct(x.shape, F32),
        compiler_params=pltpu.CompilerParams(
            dimension_semantics=("parallel",),
            vmem_limit_bytes=VMEM_LIMIT_MLP),
        name="mlp",
    )(x, gpre, gpost, wup, wdn)


def kernel(x_prompt, x_sample, state_ret, state_hgrn, lb_logits, w_in, w_a, w_b, w_out,
           w_up, w_down, g_pre_mix, g_post_mix, g_pre_ffn, g_post_ffn):
    depth = w_in.shape[0]
    nb, t_len, d = x_prompt.shape
    nbs, ts, _ = x_sample.shape
    xp, xs = x_prompt, x_sample
    ret_p, hg_p, ret_s, hg_s = [], [], [], []
    for l in range(depth):
        win = w_in[l].astype(BF16).reshape(d, -1, GROUP).transpose(1, 0, 2)
        mix_w = (lb_logits.astype(F32), g_pre_mix[l][None], g_post_mix[l][None],
                 win, w_a[l].astype(BF16), w_b[l].astype(BF16), w_out[l].astype(BF16))
        mlp_w = (g_pre_ffn[l][None], g_post_ffn[l][None],
                 w_up[l].astype(BF16), w_down[l].astype(BF16))

        xs, sr2, sh2 = _mixer(xs, PAST_LEN + jnp.arange(ts, dtype=jnp.int32),
                              (state_ret[l].astype(F32), state_hgrn[l].astype(F32)),
                              ts, ts, l, mix_w)
        xs = _mlp(xs.reshape(nbs * ts, d), nbs * ts, mlp_w).reshape(nbs, ts, d)
        ret_s.append(sr2)
        hg_s.append(sh2)

        xp, sr, sh = _mixer(xp, jnp.arange(t_len, dtype=jnp.int32), None, CHUNK,
                            PROMPT_TILE, l, mix_w)
        xp = _mlp(xp.reshape(nb * t_len, d), MLP_ROWS, mlp_w).reshape(nb, t_len, d)
        ret_p.append(sr)
        hg_p.append(sh)
    return (xp, xs,
            jnp.stack(ret_p).astype(x_prompt.dtype), jnp.stack(hg_p).astype(x_prompt.dtype),
            jnp.stack(ret_s).astype(state_ret.dtype), jnp.stack(hg_s).astype(state_hgrn.dtype))
```

```python
import functools

import numpy as np
import jax
import jax.numpy as jnp
from jax import lax
from jax.experimental import pallas as pl
from jax.experimental.pallas import tpu as pltpu

F32 = jnp.float32
BF16 = jnp.bfloat16

D_MODEL = 1024
RET_HEADS = 4
RET_DK = 256
HG_HEADS = 8
HG_DK = 128
CHUNK = 64
PAST_LEN = 2048
ROPE_BASE = 10000.0
EPS = 1e-6

V7X_LANES = 128
V7X_SUBLANES = 8
GROUP = 256
DIAG_LEVEL = 64
LOG2E = 1.4426950408889634

PROMPT_TILE = 256
MLP_ROWS = 1024
VMEM_LIMIT_MIXER = 50 * 1024 * 1024
VMEM_LIMIT_MLP = 52 * 1024 * 1024


def _dot(a, b):
    return jnp.dot(a, b, preferred_element_type=F32)


def _dot_tn(a, b):
    return lax.dot_general(a, b, (((0,), (0,)), ((), ())), preferred_element_type=F32)


def _transposed_bf16(a):
    return a.T.astype(BF16)


def _sigmoid(x):
    return 0.5 * jnp.tanh(0.5 * x) + 0.5


def _silu(x):
    hx = 0.5 * x
    return hx * jnp.tanh(hx) + hx


def _rms(x, g):
    return x * lax.rsqrt(jnp.mean(x * x, axis=-1, keepdims=True) + EPS) * g


def _head_rms(x):
    return x * lax.rsqrt(jnp.mean(x * x, axis=-1, keepdims=True) + EPS)


def _shift_down(z, s):
    n = z.shape[0]
    if s % V7X_SUBLANES == 0:
        return jnp.concatenate([z[n - s:], z[:n - s]], axis=0)
    return pltpu.roll(z, s, axis=0)


def _row_blocks(a, size):
    return [a[i * size:(i + 1) * size] for i in range(a.shape[0] // size)]


def _pair_block(tt):
    return V7X_LANES if tt > V7X_LANES and tt % V7X_LANES == 0 else tt


def _mixer_kernel(*refs, tt, layer, zero_init):
    (x_ref, cos_ref, sin_ref, dmask_ref, xi_ref, zeta_ref, dec_ref, lv_ref,
     lbl_ref, gpre_ref, gpost_ref, win_ref, wa_ref, wb_ref, wout_ref) = refs[:15]
    x1_ref, sret_ref, shg_ref, h_scr = refs[-4:]
    half = RET_DK // 2
    n_pairs = HG_HEADS // 2
    hg_base = 4 * RET_HEADS
    m_base = hg_base + 4 * n_pairs
    n_m = D_MODEL // GROUP
    blk = _pair_block(tt)
    nb = tt // blk

    @pl.when(pl.program_id(1) == 0)
    def _():
        if zero_init:
            sret_ref[...] = jnp.zeros(sret_ref.shape, F32)
            shg_ref[...] = jnp.zeros(shg_ref.shape, F32)
        else:
            sret0_ref, shg0_ref = refs[15:17]
            sret_ref[...] = sret0_ref[...]
            shg_ref[...] = shg0_ref[...]

    h_scr[...] = _rms(x_ref[0], gpre_ref[...]).astype(BF16)

    def proj(g):
        return _dot(h_scr[...], win_ref[g])

    def rope(u):
        cos, sin = cos_ref[...], sin_ref[...]
        u1, u2 = u[:, :half], u[:, half:]
        return u1 * cos - u2 * sin, u1 * sin + u2 * cos

    def ret_unit(hd):
        st = {}

        def p_q():
            st["q"] = proj(hd)

        def p_k():
            st["k"] = proj(RET_HEADS + hd)

        def p_v():
            st["vb"] = proj(2 * RET_HEADS + hd).astype(BF16)

        def p_g():
            st["g"] = proj(3 * RET_HEADS + hd)

        def scores():
            q1, q2 = rope(st.pop("q"))
            k1, k2 = rope(st.pop("k") * (RET_DK ** -0.5))
            xi = xi_ref[hd]
            zeta = zeta_ref[hd]
            qb = jnp.concatenate([q1, q2], axis=1).astype(BF16)
            kt = jnp.concatenate([_transposed_bf16(k1), _transposed_bf16(k2)], axis=0)
            st["qx"] = jnp.concatenate([q1 * xi, q2 * xi], axis=1).astype(BF16)
            st["kz"] = jnp.concatenate([k1 * zeta, k2 * zeta], axis=1).astype(BF16)
            st["s"] = _dot(qb, kt)

        def a_intra():
            sc = (st.pop("s") * dmask_ref[hd]).astype(BF16)
            st["o"] = _dot(sc, st["vb"])

        def a_inter():
            st["s_old"] = sret_ref[0, hd]
            st["o"] = st["o"] + _dot(st.pop("qx"), st["s_old"].astype(BF16))

        def a_state():
            sret_ref[0, hd] = (st.pop("s_old") * dec_ref[hd]
                               + _dot_tn(st.pop("kz"), st.pop("vb")))

        def out():
            g = st.pop("g")
            y = (_silu(g) * _head_rms(st.pop("o"))).astype(BF16)
            st["y"] = _dot(y, wa_ref[hd * RET_DK:(hd + 1) * RET_DK, :])

        return [[p_q, p_k, p_v, p_g], [scores], [a_intra, a_inter, a_state], [out]], st

    def hg_unit(p):
        st = {}

        def p_q():
            st["q"] = proj(hg_base + p)

        def p_f():
            st["fz"] = proj(hg_base + n_pairs + p)

        def p_v():
            st["vb"] = proj(hg_base + 2 * n_pairs + p).astype(BF16)

        def p_g():
            st["g"] = proj(hg_base + 3 * n_pairs + p)

        def decay():
            row = lax.broadcasted_iota(jnp.int32, (tt, GROUP), 0)
            lbl = lbl_ref[:, p * GROUP:(p + 1) * GROUP]
            lexp = jnp.exp(lbl - jnp.max(lbl, axis=0, keepdims=True))
            lb = (jnp.sum(lexp[0:layer + 1], axis=0, keepdims=True)
                  / jnp.sum(lexp, axis=0, keepdims=True))
            f = lb + (1.0 - lb) / (1.0 + jnp.exp(-st.pop("fz")))
            k = 1.0 - f
            b = jnp.log(f)
            s = 1
            while s < tt:
                if s % V7X_SUBLANES == 0:
                    b = b + jnp.concatenate([jnp.zeros((s, GROUP), F32), b[:tt - s]], axis=0)
                else:
                    b = b + jnp.where(row >= s, _shift_down(b, s), 0.0)
                s *= 2
            st["k"], st["b"], st["z"] = k, b, b
            st["diag"] = [[jnp.zeros((blk, blk), F32) for _ in range(nb)] for _ in range(2)]
            st["off"] = [{} for _ in range(2)]

        def level_step(level):
            size = 1 << level

            def step():
                lvb = lv_ref[...]
                q, k, b, z = st["q"], st["k"], st["b"], st["z"]
                diag_blk, off_blk = st["diag"], st["off"]
                if size % V7X_SUBLANES == 0:
                    n_pair = tt // (2 * size)
                    zb, qs, ks, bs = (_row_blocks(a, size) for a in (z, q, k, b))
                    arg = jnp.concatenate(
                        [(bs[2 * g + o] - zb[2 * g]) * (LOG2E if o else -LOG2E)
                         for g in range(n_pair) for o in range(2)], axis=0)
                    qk = jnp.concatenate(
                        [blk for g in range(n_pair) for blk in (ks[2 * g], qs[2 * g + 1])], axis=0)
                    z_next = jnp.concatenate(
                        [zb[2 * g + 1] for g in range(n_pair) for _ in range(2)], axis=0)
                else:
                    groups = (tt // V7X_SUBLANES, V7X_SUBLANES, GROUP)
                    sub = lax.broadcasted_iota(jnp.int32, (1,) + groups[1:], 1)
                    odd = (sub & size) != 0
                    z3, q3, k3, b3 = (a.reshape(groups) for a in (z, q, k, b))
                    bref = jnp.where(odd, pltpu.roll(z3, size, axis=1), z3)
                    arg = ((b3 - bref) * jnp.where(odd, LOG2E, -LOG2E)).reshape(tt, GROUP)
                    qk = jnp.where(odd, q3, k3).reshape(tt, GROUP)
                    z_next = jnp.where(odd, z3, pltpu.roll(z3, V7X_SUBLANES - size, axis=1)
                                       ).reshape(tt, GROUP)
                wf = qk * jnp.exp2(arg)
                w = wf.astype(BF16)
                for hh in range(2):
                    sl = slice(hh * HG_DK, (hh + 1) * HG_DK)
                    if size < blk:
                        for i in range(nb):
                            rows = slice(i * blk, (i + 1) * blk)
                            diag_blk[hh][i] = jnp.where(
                                lvb == level, _dot(w[rows, sl], _transposed_bf16(wf[rows, sl])),
                                diag_blk[hh][i])
                    else:
                        for i in range(nb):
                            for j in range(i):
                                if ((i * blk) ^ (j * blk)).bit_length() - 1 == level:
                                    off_blk[hh][(i, j)] = _dot(
                                        w[i * blk:(i + 1) * blk, sl],
                                        _transposed_bf16(wf[j * blk:(j + 1) * blk, sl]))
                st["z"] = z_next

            return step

        def assemble():
            lvb = lv_ref[...]
            q, k, b = st.pop("q"), st.pop("k"), st.pop("b")
            diag_blk, off_blk = st.pop("diag"), st.pop("off")
            st.pop("z")
            qb = q.astype(BF16)
            for hh in range(2):
                sl = slice(hh * HG_DK, (hh + 1) * HG_DK)
                for i in range(nb):
                    rows = slice(i * blk, (i + 1) * blk)
                    d = jnp.where(lvb == DIAG_LEVEL,
                                  _dot(qb[rows, sl], _transposed_bf16(k[rows, sl])),
                                  diag_blk[hh][i])
                    parts = [off_blk[hh][(i, j)] for j in range(i)] + [d]
                    st["a%d_%d" % (hh, i)] = jnp.concatenate(parts, axis=1).astype(BF16)
            b_last = b[tt - 1:tt, :]
            st["qe"] = (q * jnp.exp(b)).astype(BF16)
            st["kd"] = (k * jnp.exp(b_last - b)).astype(BF16)
            st["e_last"] = jnp.exp(b_last)
            st["outs"] = []

        def attend_step(hh):
            sl = slice(hh * HG_DK, (hh + 1) * HG_DK)

            def read():
                st["s_old"] = shg_ref[0, 2 * p + hh]
                intra = [_dot(st.pop("a%d_%d" % (hh, i)), st["vb"][:(i + 1) * blk, sl])
                         for i in range(nb)]
                o = (jnp.concatenate(intra, axis=0)
                     + _dot(st["qe"][:, sl], st["s_old"].astype(BF16)))
                st["outs"].append(_head_rms(o))

            def update():
                scale = jnp.broadcast_to(st["e_last"][:, sl], (HG_DK, HG_DK)).T
                shg_ref[0, 2 * p + hh] = (st.pop("s_old") * scale
                                          + _dot_tn(st["kd"][:, sl], st["vb"][:, sl]))

            return [read, update]

        def out():
            for name in ("vb", "qe", "kd", "e_last"):
                st.pop(name)
            y = (_silu(st.pop("g")) * jnp.concatenate(st.pop("outs"), axis=1)).astype(BF16)
            st["y"] = _dot(y, wb_ref[p * GROUP:(p + 1) * GROUP, :])

        n_level = tt.bit_length() - 1
        stages = [[p_q, p_f, p_v, p_g],
                  [decay] + [level_step(l) for l in range(n_level)] + [assemble],
                  attend_step(0) + attend_step(1),
                  [out]]
        return stages, st

    def gate_unit(c):
        st = {}

        def p_a():
            st["ga"] = _sigmoid(proj(m_base + c))

        def p_b():
            st["gb"] = _sigmoid(proj(m_base + n_m + c))

        return [[p_a, p_b]], st

    def emit_interleaved(stages):
        keyed = []
        for si, steps in enumerate(stages):
            for j, fn in enumerate(steps):
                keyed.append(((j + 0.5) / len(steps), si, fn))
        for _, _, fn in sorted(keyed, key=lambda item: item[:2]):
            fn()

    ret_units = [ret_unit(i) for i in range(RET_HEADS)]
    hg_units = [hg_unit(i) for i in range(n_pairs)]
    units = [u for pair in zip(ret_units, hg_units) for u in pair]
    n_branch = len(units)
    gates = [gate_unit(c) for c in range(n_m)]
    n_stage = 4
    n_tau = n_branch + n_stage - 1
    gate_at = {n_tau - n_m + c: c for c in range(n_m)}
    for tau in range(n_tau):
        active = [units[u][0][tau - u] for u in range(n_branch) if 0 <= tau - u < n_stage]
        if tau in gate_at:
            active.append(gates[gate_at[tau]][0][0])
        emit_interleaved(active)

    y_a = sum(u[1]["y"] for u in ret_units)
    y_b = sum(u[1]["y"] for u in hg_units)
    cols = []
    for c in range(n_m):
        sl = slice(c * GROUP, (c + 1) * GROUP)
        cols.append(gates[c][1]["ga"] * y_a[:, sl] + gates[c][1]["gb"] * y_b[:, sl])
    mix = jnp.concatenate(cols, axis=1).astype(BF16)
    x1_ref[0] = x_ref[0] + _rms(_dot(mix, wout_ref[...]), gpost_ref[...])


def _mlp_kernel(x_ref, gpre_ref, gpost_ref, wup_ref, wdn_ref, o_ref):
    x = x_ref[...]
    h = _rms(x, gpre_ref[...]).astype(BF16)
    acc = jnp.zeros(x.shape, F32)
    d = x.shape[1]
    for c in range(wup_ref.shape[1] // d):
        u = jnp.maximum(_dot(h, wup_ref[:, c * d:(c + 1) * d]), 0.0)
        acc = acc + _dot((u * u).astype(BF16), wdn_ref[c * d:(c + 1) * d, :])
    o_ref[...] = x + _rms(acc, gpost_ref[...])


def _ret_log_gamma():
    g = 1.0 - np.exp(np.linspace(np.log(1.0 / 32), np.log(1.0 / 512), RET_HEADS))
    return jnp.asarray(np.log(g), dtype=F32)


def _tile_tables(tt, chunk):
    lg = _ret_log_gamma()
    idx = jnp.arange(tt, dtype=F32)
    dist = jnp.abs(idx[:, None] - idx[None, :])
    ch = np.arange(tt) // chunk
    allowed = jnp.asarray(ch[None, :] <= ch[:, None])
    dmask = jnp.where(allowed[None], jnp.exp(dist[None] * lg[:, None, None]), 0.0)
    xi = jnp.exp((idx + 1.0)[None, :] * lg[:, None])
    zeta = jnp.exp((tt - 1.0 - idx)[None, :] * lg[:, None])
    xi = jnp.broadcast_to(xi[:, :, None], (RET_HEADS, tt, V7X_LANES))
    zeta = jnp.broadcast_to(zeta[:, :, None], (RET_HEADS, tt, V7X_LANES))
    dec = jnp.broadcast_to(jnp.exp(tt * lg)[:, None, None], (RET_HEADS, 1, RET_DK))
    t = np.arange(_pair_block(tt))
    x = t[:, None] ^ t[None, :]
    lvl = np.floor(np.log2(np.maximum(x, 1))).astype(np.int32)
    lvl = np.where(t[:, None] > t[None, :], lvl, -1)
    lvl = np.where(t[:, None] == t[None, :], DIAG_LEVEL, lvl).astype(np.int32)
    return dmask, xi, zeta, dec, jnp.asarray(lvl)


def _rope_tables(pos):
    inv = ROPE_BASE ** (-jnp.arange(0, RET_DK, 2, dtype=F32) / RET_DK)
    ang = pos.astype(F32)[:, None] * inv[None, :]
    return jnp.cos(ang), jnp.sin(ang)


def _mixer(x, pos, states, chunk, tt, layer, weights):
    lbl, gpre, gpost, win, wa, wb, wout = weights
    nb, t_len, d = x.shape
    assert t_len % tt == 0 and tt % (2 * V7X_SUBLANES) == 0 and tt & (tt - 1) == 0, (t_len, tt)
    nt = t_len // tt
    cos, sin = _rope_tables(pos)
    dmask, xi, zeta, dec, lvl = _tile_tables(tt, chunk)
    whole = pl.BlockSpec(memory_space=pltpu.VMEM)
    ret_shape = (nb, RET_HEADS, RET_DK, RET_DK)
    hg_shape = (nb, HG_HEADS, HG_DK, HG_DK)
    ret_state = pl.BlockSpec((1,) + ret_shape[1:], lambda b, t: (b, 0, 0, 0))
    hg_state = pl.BlockSpec((1,) + hg_shape[1:], lambda b, t: (b, 0, 0, 0))
    x_spec = pl.BlockSpec((1, tt, d), lambda b, t: (b, t, 0))
    rope_spec = pl.BlockSpec((tt, V7X_LANES), lambda b, t: (t, 0))
    state_specs = [] if states is None else [ret_state, hg_state]
    state_args = () if states is None else tuple(states)
    return pl.pallas_call(
        functools.partial(_mixer_kernel, tt=tt, layer=layer, zero_init=states is None),
        grid=(nb, nt),
        in_specs=[x_spec, rope_spec, rope_spec] + [whole] * 12 + state_specs,
        out_specs=[x_spec, ret_state, hg_state],
        out_shape=[jax.ShapeDtypeStruct(x.shape, F32),
                   jax.ShapeDtypeStruct(ret_shape, F32),
                   jax.ShapeDtypeStruct(hg_shape, F32)],
        scratch_shapes=[pltpu.VMEM((tt, d), BF16)],
        compiler_params=pltpu.CompilerParams(
            dimension_semantics=("parallel", "arbitrary"),
            vmem_limit_bytes=VMEM_LIMIT_MIXER),
        name="mixer",
    )(x, cos, sin, dmask, xi, zeta, dec, lvl, lbl, gpre, gpost, win, wa, wb, wout,
      *state_args)


def _mlp(x, rows, weights):
    gpre, gpost, wup, wdn = weights
    n, d = x.shape
    whole = pl.BlockSpec(memory_space=pltpu.VMEM)
    x_spec = pl.BlockSpec((rows, d), lambda i: (i, 0))
    return pl.pallas_call(
        _mlp_kernel,
        grid=(n // rows,),
        in_specs=[x_spec] + [whole] * 4,
        out_specs=x_spec,
        out_shape=jax.ShapeDtypeStruct(x.shape, F32),
        compiler_params=pltpu.CompilerParams(
            dimension_semantics=("parallel",),
            vmem_limit_bytes=VMEM_LIMIT_MLP),
        name="mlp",
    )(x, gpre, gpost, wup, wdn)


def kernel(x_prompt, x_sample, state_ret, state_hgrn, lb_logits, w_in, w_a, w_b, w_out,
           w_up, w_down, g_pre_mix, g_post_mix, g_pre_ffn, g_post_ffn):
    depth = w_in.shape[0]
    nb, t_len, d = x_prompt.shape
    nbs, ts, _ = x_sample.shape
    xp, xs = x_prompt, x_sample
    ret_p, hg_p, ret_s, hg_s = [], [], [], []
    for l in range(depth):
        win = w_in[l].astype(BF16).reshape(d, -1, GROUP).transpose(1, 0, 2)
        mix_w = (lb_logits.astype(F32), g_pre_mix[l][None], g_post_mix[l][None],
                 win, w_a[l].astype(BF16), w_b[l].astype(BF16), w_out[l].astype(BF16))
        mlp_w = (g_pre_ffn[l][None], g_post_ffn[l][None],
                 w_up[l].astype(BF16), w_down[l].astype(BF16))

        xs, sr2, sh2 = _mixer(xs, PAST_LEN + jnp.arange(ts, dtype=jnp.int32),
                              (state_ret[l].astype(F32), state_hgrn[l].astype(F32)),
                              ts, ts, l, mix_w)
        xs = _mlp(xs.reshape(nbs * ts, d), nbs * ts, mlp_w).reshape(nbs, ts, d)
        ret_s.append(sr2)
        hg_s.append(sh2)

        xp, sr, sh = _mixer(xp, jnp.arange(t_len, dtype=jnp.int32), None, CHUNK,
                            PROMPT_TILE, l, mix_w)
        xp = _mlp(xp.reshape(nb * t_len, d), MLP_ROWS, mlp_w).reshape(nb, t_len, d)
        ret_p.append(sr)
        hg_p.append(sh)
    return (xp, xs,
            jnp.stack(ret_p).astype(x_prompt.dtype), jnp.stack(hg_p).astype(x_prompt.dtype),
            jnp.stack(ret_s).astype(state_ret.dtype), jnp.stack(hg_s).astype(state_hgrn.dtype))
```

```python
import functools

import numpy as np
import jax
import jax.numpy as jnp
from jax import lax
from jax.experimental import pallas as pl
from jax.experimental.pallas import tpu as pltpu

F32 = jnp.float32
BF16 = jnp.bfloat16

D_MODEL = 1024
RET_HEADS = 4
RET_DK = 256
HG_HEADS = 8
HG_DK = 128
CHUNK = 64
PAST_LEN = 2048
ROPE_BASE = 10000.0
EPS = 1e-6

V7X_LANES = 128
V7X_SUBLANES = 8
GROUP = 256
DIAG_LEVEL = 64
LOG2E = 1.4426950408889634

PROMPT_TILE = 256
MLP_ROWS = 1024
VMEM_LIMIT_MIXER = 50 * 1024 * 1024
VMEM_LIMIT_MLP = 52 * 1024 * 1024


def _dot(a, b):
    return jnp.dot(a, b, preferred_element_type=F32)


def _dot_tn(a, b):
    return lax.dot_general(a, b, (((0,), (0,)), ((), ())), preferred_element_type=F32)


def _transposed_bf16(a):
    return a.T.astype(BF16)


def _sigmoid(x):
    return 0.5 * jnp.tanh(0.5 * x) + 0.5


def _silu(x):
    hx = 0.5 * x
    return hx * jnp.tanh(hx) + hx


def _rms(x, g):
    return x * lax.rsqrt(jnp.mean(x * x, axis=-1, keepdims=True) + EPS) * g


def _head_rms(x):
    return x * lax.rsqrt(jnp.mean(x * x, axis=-1, keepdims=True) + EPS)


def _shift_down(z, s):
    n = z.shape[0]
    if s % V7X_SUBLANES == 0:
        return jnp.concatenate([z[n - s:], z[:n - s]], axis=0)
    return pltpu.roll(z, s, axis=0)


def _row_blocks(a, size):
    return [a[i * size:(i + 1) * size] for i in range(a.shape[0] // size)]


def _pair_block(tt):
    return V7X_LANES if tt > V7X_LANES and tt % V7X_LANES == 0 else tt


def _mixer_kernel(*refs, tt, layer, zero_init):
    (x_ref, cos_ref, sin_ref, dmask_ref, xi_ref, zeta_ref, dec_ref, lv_ref,
     lbl_ref, gpre_ref, gpost_ref, win_ref, wa_ref, wb_ref, wout_ref) = refs[:15]
    x1_ref, sret_ref, shg_ref, h_scr = refs[-4:]
    half = RET_DK // 2
    n_pairs = HG_HEADS // 2
    hg_base = 4 * RET_HEADS
    m_base = hg_base + 4 * n_pairs
    n_m = D_MODEL // GROUP
    blk = _pair_block(tt)
    nb = tt // blk

    @pl.when(pl.program_id(1) == 0)
    def _():
        if zero_init:
            sret_ref[...] = jnp.zeros(sret_ref.shape, F32)
            shg_ref[...] = jnp.zeros(shg_ref.shape, F32)
        else:
            sret0_ref, shg0_ref = refs[15:17]
            sret_ref[...] = sret0_ref[...]
            shg_ref[...] = shg0_ref[...]

    h_scr[...] = _rms(x_ref[0], gpre_ref[...]).astype(BF16)

    def proj(g):
        return _dot(h_scr[...], win_ref[g])

    def rope(u):
        cos, sin = cos_ref[...], sin_ref[...]
        u1, u2 = u[:, :half], u[:, half:]
        return u1 * cos - u2 * sin, u1 * sin + u2 * cos

    def ret_unit(hd):
        st = {}

        def p_q():
            st["q"] = proj(hd)

        def p_k():
            st["k"] = proj(RET_HEADS + hd)

        def p_v():
            st["vb"] = proj(2 * RET_HEADS + hd).astype(BF16)

        def p_g():
            st["g"] = proj(3 * RET_HEADS + hd)

        def scores():
            q1, q2 = rope(st.pop("q"))
            k1, k2 = rope(st.pop("k") * (RET_DK ** -0.5))
            xi = xi_ref[hd]
            zeta = zeta_ref[hd]
            qb = jnp.concatenate([q1, q2], axis=1).astype(BF16)
            kt = jnp.concatenate([_transposed_bf16(k1), _transposed_bf16(k2)], axis=0)
            st["qx"] = jnp.concatenate([q1 * xi, q2 * xi], axis=1).astype(BF16)
            st["kz"] = jnp.concatenate([k1 * zeta, k2 * zeta], axis=1).astype(BF16)
            st["s"] = _dot(qb, kt)

        def a_intra():
            sc = (st.pop("s") * dmask_ref[hd]).astype(BF16)
            st["o"] = _dot(sc, st["vb"])

        def a_inter():
            st["s_old"] = sret_ref[0, hd]
            st["o"] = st["o"] + _dot(st.pop("qx"), st["s_old"].astype(BF16))

        def a_state():
            sret_ref[0, hd] = (st.pop("s_old") * dec_ref[hd]
                               + _dot_tn(st.pop("kz"), st.pop("vb")))

        def out():
            g = st.pop("g")
            y = (_silu(g) * _head_rms(st.pop("o"))).astype(BF16)
            st["y"] = _dot(y, wa_ref[hd * RET_DK:(hd + 1) * RET_DK, :])

        return [[p_q, p_k, p_v, p_g], [scores], [a_intra, a_inter, a_state], [out]], st

    def hg_unit(p):
        st = {}

        def p_q():
            st["q"] = proj(hg_base + p)

        def p_f():
            st["fz"] = proj(hg_base + n_pairs + p)

        def p_v():
            st["vb"] = proj(hg_base + 2 * n_pairs + p).astype(BF16)

        def p_g():
            st["g"] = proj(hg_base + 3 * n_pairs + p)

        def decay():
            lbl = lbl_ref[:, p * GROUP:(p + 1) * GROUP]
            lexp = jnp.exp(lbl - jnp.max(lbl, axis=0, keepdims=True))
            lb = (jnp.sum(lexp[0:layer + 1], axis=0, keepdims=True)
                  / jnp.sum(lexp, axis=0, keepdims=True))
            f = lb + (1.0 - lb) / (1.0 + jnp.exp(-st.pop("fz")))
            k = 1.0 - f
            row = lax.broadcasted_iota(jnp.int32, (tt, GROUP), 0)
            b = jnp.log(f)
            s = 1
            while s < tt:
                if s % V7X_SUBLANES == 0:
                    b = b + jnp.concatenate([jnp.zeros((s, GROUP), F32), b[:tt - s]], axis=0)
                else:
                    b = b + jnp.where(row >= s, _shift_down(b, s), 0.0)
                s *= 2
            st["k"], st["b"], st["z"] = k, b, b
            st["diag"] = [[jnp.zeros((blk, blk), F32) for _ in range(nb)] for _ in range(2)]
            st["off"] = [{} for _ in range(2)]

        def level_step(level):
            size = 1 << level

            def step():
                lvb = lv_ref[...]
                q, k, b, z = st["q"], st["k"], st["b"], st["z"]
                diag_blk, off_blk = st["diag"], st["off"]
                if size % V7X_SUBLANES == 0:
                    n_pair = tt // (2 * size)
                    zb, qs, ks, bs = (_row_blocks(a, size) for a in (z, q, k, b))
                    arg = jnp.concatenate(
                        [(bs[2 * g + o] - zb[2 * g]) * (LOG2E if o else -LOG2E)
                         for g in range(n_pair) for o in range(2)], axis=0)
                    qk = jnp.concatenate(
                        [blk for g in range(n_pair) for blk in (ks[2 * g], qs[2 * g + 1])], axis=0)
                    z_next = jnp.concatenate(
                        [zb[2 * g + 1] for g in range(n_pair) for _ in range(2)], axis=0)
                else:
                    groups = (tt // V7X_SUBLANES, V7X_SUBLANES, GROUP)
                    sub = lax.broadcasted_iota(jnp.int32, (1,) + groups[1:], 1)
                    odd = (sub & size) != 0
                    z3, q3, k3, b3 = (a.reshape(groups) for a in (z, q, k, b))
                    bref = jnp.where(odd, pltpu.roll(z3, size, axis=1), z3)
                    arg = ((b3 - bref) * jnp.where(odd, LOG2E, -LOG2E)).reshape(tt, GROUP)
                    qk = jnp.where(odd, q3, k3).reshape(tt, GROUP)
                    z_next = jnp.where(odd, z3, pltpu.roll(z3, V7X_SUBLANES - size, axis=1)
                                       ).reshape(tt, GROUP)
                wf = qk * jnp.exp2(arg)
                w = wf.astype(BF16)
                for hh in range(2):
                    sl = slice(hh * HG_DK, (hh + 1) * HG_DK)
                    if size < blk:
                        for i in range(nb):
                            rows = slice(i * blk, (i + 1) * blk)
                            diag_blk[hh][i] = jnp.where(
                                lvb == level, _dot(w[rows, sl], _transposed_bf16(wf[rows, sl])),
                                diag_blk[hh][i])
                    else:
                        for i in range(nb):
                            for j in range(i):
                                if ((i * blk) ^ (j * blk)).bit_length() - 1 == level:
                                    off_blk[hh][(i, j)] = _dot(
                                        w[i * blk:(i + 1) * blk, sl],
                                        _transposed_bf16(wf[j * blk:(j + 1) * blk, sl]))
                st["z"] = z_next

            return step

        def assemble():
            lvb = lv_ref[...]
            q, k, b = st.pop("q"), st.pop("k"), st.pop("b")
            diag_blk, off_blk = st.pop("diag"), st.pop("off")
            st.pop("z")
            qb = q.astype(BF16)
            for hh in range(2):
                sl = slice(hh * HG_DK, (hh + 1) * HG_DK)
                for i in range(nb):
                    rows = slice(i * blk, (i + 1) * blk)
                    d = jnp.where(lvb == DIAG_LEVEL,
                                  _dot(qb[rows, sl], _transposed_bf16(k[rows, sl])),
                                  diag_blk[hh][i])
                    parts = [off_blk[hh][(i, j)] for j in range(i)] + [d]
                    st["a%d_%d" % (hh, i)] = jnp.concatenate(parts, axis=1).astype(BF16)
            b_last = b[tt - 1:tt, :]
            st["qe"] = (q * jnp.exp(b)).astype(BF16)
            st["kd"] = (k * jnp.exp(b_last - b)).astype(BF16)
            st["e_last"] = jnp.exp(b_last)
            st["outs"] = []

        def attend_step(hh):
            sl = slice(hh * HG_DK, (hh + 1) * HG_DK)

            def read():
                st["s_old"] = shg_ref[0, 2 * p + hh]
                intra = [_dot(st.pop("a%d_%d" % (hh, i)), st["vb"][:(i + 1) * blk, sl])
                         for i in range(nb)]
                o = (jnp.concatenate(intra, axis=0)
                     + _dot(st["qe"][:, sl], st["s_old"].astype(BF16)))
                st["outs"].append(_head_rms(o))

            def update():
                scale = jnp.broadcast_to(st["e_last"][:, sl], (HG_DK, HG_DK)).T
                shg_ref[0, 2 * p + hh] = (st.pop("s_old") * scale
                                          + _dot_tn(st["kd"][:, sl], st["vb"][:, sl]))

            return [read, update]

        def out():
            for name in ("vb", "qe", "kd", "e_last"):
                st.pop(name)
            y = (_silu(st.pop("g")) * jnp.concatenate(st.pop("outs"), axis=1)).astype(BF16)
            st["y"] = _dot(y, wb_ref[p * GROUP:(p + 1) * GROUP, :])

        n_level = tt.bit_length() - 1
        stages = [[p_q, p_f, p_v, p_g],
                  [decay] + [level_step(l) for l in range(n_level)] + [assemble],
                  attend_step(0) + attend_step(1),
                  [out]]
        return stages, st

    def gate_unit(c):
        st = {}

        def p_a():
            st["ga"] = _sigmoid(proj(m_base + c))

        def p_b():
            st["gb"] = _sigmoid(proj(m_base + n_m + c))

        return [[p_a, p_b]], st

    def emit_interleaved(stages):
        keyed = []
        for si, steps in enumerate(stages):
            for j, fn in enumerate(steps):
                keyed.append(((j + 0.5) / len(steps), si, fn))
        for _, _, fn in sorted(keyed, key=lambda item: item[:2]):
            fn()

    ret_units = [ret_unit(i) for i in range(RET_HEADS)]
    hg_units = [hg_unit(i) for i in range(n_pairs)]
    units = [u for pair in zip(ret_units, hg_units) for u in pair]
    n_branch = len(units)
    gates = [gate_unit(c) for c in range(n_m)]
    n_stage = 4
    n_tau = n_branch + n_stage - 1
    gate_at = {n_tau - n_m + c: c for c in range(n_m)}
    for tau in range(n_tau):
        active = [units[u][0][tau - u] for u in range(n_branch) if 0 <= tau - u < n_stage]
        if tau in gate_at:
            active.append(gates[gate_at[tau]][0][0])
        emit_interleaved(active)

    y_a = sum(u[1]["y"] for u in ret_units)
    y_b = sum(u[1]["y"] for u in hg_units)
    cols = []
    for c in range(n_m):
        sl = slice(c * GROUP, (c + 1) * GROUP)
        cols.append(gates[c][1]["ga"] * y_a[:, sl] + gates[c][1]["gb"] * y_b[:, sl])
    mix = jnp.concatenate(cols, axis=1).astype(BF16)
    x1_ref[0] = x_ref[0] + _rms(_dot(mix, wout_ref[...]), gpost_ref[...])


def _mlp_kernel(x_ref, gpre_ref, gpost_ref, wup_ref, wdn_ref, o_ref):
    x = x_ref[...]
    h = _rms(x, gpre_ref[...]).astype(BF16)
    acc = jnp.zeros(x.shape, F32)
    d = x.shape[1]
    for c in range(wup_ref.shape[1] // d):
        u = jnp.maximum(_dot(h, wup_ref[:, c * d:(c + 1) * d]), 0.0)
        acc = acc + _dot((u * u).astype(BF16), wdn_ref[c * d:(c + 1) * d, :])
    o_ref[...] = x + _rms(acc, gpost_ref[...])


def _ret_log_gamma():
    g = 1.0 - np.exp(np.linspace(np.log(1.0 / 32), np.log(1.0 / 512), RET_HEADS))
    return jnp.asarray(np.log(g), dtype=F32)


def _tile_tables(tt, chunk):
    lg = _ret_log_gamma()
    idx = jnp.arange(tt, dtype=F32)
    dist = jnp.abs(idx[:, None] - idx[None, :])
    ch = np.arange(tt) // chunk
    allowed = jnp.asarray(ch[None, :] <= ch[:, None])
    dmask = jnp.where(allowed[None], jnp.exp(dist[None] * lg[:, None, None]), 0.0)
    xi = jnp.exp((idx + 1.0)[None, :] * lg[:, None])
    zeta = jnp.exp((tt - 1.0 - idx)[None, :] * lg[:, None])
    xi = jnp.broadcast_to(xi[:, :, None], (RET_HEADS, tt, V7X_LANES))
    zeta = jnp.broadcast_to(zeta[:, :, None], (RET_HEADS, tt, V7X_LANES))
    dec = jnp.broadcast_to(jnp.exp(tt * lg)[:, None, None], (RET_HEADS, 1, RET_DK))
    t = np.arange(_pair_block(tt))
    x = t[:, None] ^ t[None, :]
    lvl = np.floor(np.log2(np.maximum(x, 1))).astype(np.int32)
    lvl = np.where(t[:, None] > t[None, :], lvl, -1)
    lvl = np.where(t[:, None] == t[None, :], DIAG_LEVEL, lvl).astype(np.int32)
    return dmask, xi, zeta, dec, jnp.asarray(lvl)


def _rope_tables(pos):
    inv = ROPE_BASE ** (-jnp.arange(0, RET_DK, 2, dtype=F32) / RET_DK)
    ang = pos.astype(F32)[:, None] * inv[None, :]
    return jnp.cos(ang), jnp.sin(ang)


def _mixer(x, pos, states, chunk, tt, layer, weights):
    lbl, gpre, gpost, win, wa, wb, wout = weights
    nb, t_len, d = x.shape
    assert t_len % tt == 0 and tt % (2 * V7X_SUBLANES) == 0 and tt & (tt - 1) == 0, (t_len, tt)
    nt = t_len // tt
    cos, sin = _rope_tables(pos)
    dmask, xi, zeta, dec, lvl = _tile_tables(tt, chunk)
    whole = pl.BlockSpec(memory_space=pltpu.VMEM)
    ret_shape = (nb, RET_HEADS, RET_DK, RET_DK)
    hg_shape = (nb, HG_HEADS, HG_DK, HG_DK)
    ret_state = pl.BlockSpec((1,) + ret_shape[1:], lambda b, t: (b, 0, 0, 0))
    hg_state = pl.BlockSpec((1,) + hg_shape[1:], lambda b, t: (b, 0, 0, 0))
    x_spec = pl.BlockSpec((1, tt, d), lambda b, t: (b, t, 0))
    rope_spec = pl.BlockSpec((tt, V7X_LANES), lambda b, t: (t, 0))
    state_specs = [] if states is None else [ret_state, hg_state]
    state_args = () if states is None else tuple(states)
    return pl.pallas_call(
        functools.partial(_mixer_kernel, tt=tt, layer=layer, zero_init=states is None),
        grid=(nb, nt),
        in_specs=[x_spec, rope_spec, rope_spec] + [whole] * 12 + state_specs,
        out_specs=[x_spec, ret_state, hg_state],
        out_shape=[jax.ShapeDtypeStruct(x.shape, F32),
                   jax.ShapeDtypeStruct(ret_shape, F32),
                   jax.ShapeDtypeStruct(hg_shape, F32)],
        scratch_shapes=[pltpu.VMEM((tt, d), BF16)],
        compiler_params=pltpu.CompilerParams(
            dimension_semantics=("parallel", "arbitrary"),
            vmem_limit_bytes=VMEM_LIMIT_MIXER),
        name="mixer",
    )(x, cos, sin, dmask, xi, zeta, dec, lvl, lbl, gpre, gpost, win, wa, wb, wout,
      *state_args)


def _mlp(x, rows, weights):
    gpre, gpost, wup, wdn = weights
    n, d = x.shape
    whole = pl.BlockSpec(memory_space=pltpu.VMEM)
    x_spec = pl.BlockSpec((rows, d), lambda i: (i, 0))
    return pl.pallas_call(
        _mlp_kernel,
        grid=(n // rows,),
        in_specs=[x_spec] + [whole] * 4,
        out_specs=x_spec,
        out_shape=jax.ShapeDtypeStruct(x.shape, F32),
        compiler_params=pltpu.CompilerParams(
            dimension_semantics=("parallel",),
            vmem_limit_bytes=VMEM_LIMIT_MLP),
        name="mlp",
    )(x, gpre, gpost, wup, wdn)


def kernel(x_prompt, x_sample, state_ret, state_hgrn, lb_logits, w_in, w_a, w_b, w_out,
           w_up, w_down, g_pre_mix, g_post_mix, g_pre_ffn, g_post_ffn):
    depth = w_in.shape[0]
    nb, t_len, d = x_prompt.shape
    nbs, ts, _ = x_sample.shape
    xp, xs = x_prompt, x_sample
    ret_p, hg_p, ret_s, hg_s = [], [], [], []
    for l in range(depth):
        win = w_in[l].reshape(d, -1, GROUP).transpose(1, 0, 2).astype(BF16)
        mix_w = (lb_logits.astype(F32), g_pre_mix[l][None], g_post_mix[l][None],
                 win, w_a[l].astype(BF16), w_b[l].astype(BF16), w_out[l].astype(BF16))
        mlp_w = (g_pre_ffn[l][None], g_post_ffn[l][None],
                 w_up[l].astype(BF16), w_down[l].astype(BF16))

        xs, sr2, sh2 = _mixer(xs, PAST_LEN + jnp.arange(ts, dtype=jnp.int32),
                              (state_ret[l].astype(F32), state_hgrn[l].astype(F32)),
                              ts, ts, l, mix_w)
        xs = _mlp(xs.reshape(nbs * ts, d), nbs * ts, mlp_w).reshape(nbs, ts, d)
        ret_s.append(sr2)
        hg_s.append(sh2)

        xp, sr, sh = _mixer(xp, jnp.arange(t_len, dtype=jnp.int32), None, CHUNK,
                            PROMPT_TILE, l, mix_w)
        xp = _mlp(xp.reshape(nb * t_len, d), MLP_ROWS, mlp_w).reshape(nb, t_len, d)
        ret_p.append(sr)
        hg_p.append(sh)
    return (xp, xs,
            jnp.stack(ret_p).astype(x_prompt.dtype), jnp.stack(hg_p).astype(x_prompt.dtype),
            jnp.stack(ret_s).astype(state_ret.dtype), jnp.stack(hg_s).astype(state_hgrn.dtype))
```
